```python
import math
import jax, jax.numpy as jnp
from jax import lax
import numpy as np

D_MODEL = 1024
BATCH = 8
SEQ = 8192
DEPTH = 4

D_MIX = D_MODEL
MLA_HEADS = 4
MLA_NOPE = 128
MLA_ROPE = 64
MLA_V = 128
Q_LORA = 384
KV_LORA = 256
ROPE_BASE = 10000.0
Q_BLOCK = 128
MLA_W = MLA_HEADS * MLA_V
POOL_WINDOWS = (2, 4, 8, 16)
POOL_GROUPS = 4
POOL_CH = 64
POOL_W = POOL_GROUPS * POOL_CH
M_HEADS = 4
M_HEAD_DIM = 64
M_W = M_HEADS * M_HEAD_DIM
M_CONV = 4
M_CHUNK = 64
D_FF = 2816
FFN_CONV = 3
LN_EPS = 1e-5
RMS_EPS = 1e-6
ALPHA = (2 * DEPTH) ** 0.25
BETA = (8 * DEPTH) ** -0.25

IN_SIZES = (Q_LORA, KV_LORA, MLA_ROPE, POOL_W, M_W, M_W, M_W, M_HEADS, M_HEADS)
D_IN = Q_LORA + KV_LORA + MLA_ROPE + POOL_W + 3 * M_W + 2 * M_HEADS

kernel_name = "hymba_style_mla_pool_mlstm_deepnorm"


def _split_points():
    return [int(v) for v in np.cumsum(np.array(IN_SIZES))[:-1]]


def layer_norm(x, g, b):
    xf = x.astype(jnp.float32)
    mu = jnp.mean(xf, axis=-1, keepdims=True)
    var = jnp.mean(jnp.square(xf - mu), axis=-1, keepdims=True)
    y = (xf - mu) * lax.rsqrt(var + LN_EPS) * g + b
    return y.astype(x.dtype)


def rms_norm(x, g):
    xf = x.astype(jnp.float32)
    y = xf * lax.rsqrt(jnp.mean(jnp.square(xf), axis=-1, keepdims=True) + RMS_EPS) * g
    return y.astype(x.dtype)


def rotary_tables(seq, dim):
    inv = 1.0 / (ROPE_BASE ** (jnp.arange(0, dim, 2, dtype=jnp.float32) / dim))
    ang = jnp.arange(seq, dtype=jnp.float32)[:, None] * inv[None, :]
    return jnp.cos(ang), jnp.sin(ang)


def apply_rotary(x, cos, sin):
    xf = x.astype(jnp.float32)
    half = xf.shape[-1] // 2
    x1, x2 = xf[..., :half], xf[..., half:]
    return jnp.concatenate([x1 * cos - x2 * sin, x2 * cos + x1 * sin], axis=-1).astype(x.dtype)


def causal_dwconv(x, w, b):
    k = w.shape[0]
    y = lax.conv_general_dilated(
        x, w[:, None, :], window_strides=(1,), padding=((k - 1, 0),),
        dimension_numbers=("NWC", "WIO", "NWC"), feature_group_count=x.shape[-1])
    return y + b


def mla(cq, ckv, kr, g_qn, w_uq, g_kvn, w_ukv, cos, sin):
    B, S, _ = cq.shape
    H = MLA_HEADS
    q = (rms_norm(cq, g_qn) @ w_uq).reshape(B, S, H, MLA_NOPE + MLA_ROPE)
    q_nope, q_pe = q[..., :MLA_NOPE], q[..., MLA_NOPE:]
    q_pe = apply_rotary(q_pe, cos[:, None, :], sin[:, None, :])
    kv = (rms_norm(ckv, g_kvn) @ w_ukv).reshape(B, S, H, MLA_NOPE + MLA_V)
    k_nope, v = kv[..., :MLA_NOPE], kv[..., MLA_NOPE:]
    k_pe = apply_rotary(kr, cos, sin)
    scale = (MLA_NOPE + MLA_ROPE) ** -0.5
    nb = S // Q_BLOCK
    qn_b = (q_nope * scale).reshape(B, nb, Q_BLOCK, H, MLA_NOPE).transpose(1, 0, 3, 2, 4)
    qp_b = (q_pe * scale).reshape(B, nb, Q_BLOCK, H, MLA_ROPE).transpose(1, 0, 3, 2, 4)
    k_nope_t = k_nope.transpose(0, 2, 1, 3)
    v_t = v.transpose(0, 2, 1, 3)
    key_pos = jnp.arange(S)

    def block(args):
        qn, qp, start = args
        s = (jnp.einsum("bhqd,bhkd->bhqk", qn, k_nope_t)
             + jnp.einsum("bhqd,bkd->bhqk", qp, k_pe)).astype(jnp.float32)
        q_pos = start + jnp.arange(Q_BLOCK)
        s = jnp.where(key_pos[None, :] <= q_pos[:, None], s, -jnp.inf)
        p = jax.nn.softmax(s, axis=-1).astype(v_t.dtype)
        return jnp.einsum("bhqk,bhkd->bhqd", p, v_t)

    o = lax.map(block, (qn_b, qp_b, jnp.arange(nb) * Q_BLOCK))
    return o.transpose(1, 0, 3, 2, 4).reshape(B, S, H * MLA_V).astype(cq.dtype)


def multiscale_pool(p, w_pool, s_pool):
    B, S, _ = p.shape
    G, C = POOL_GROUPS, POOL_CH
    pg = p.reshape(B, S, G, C).astype(jnp.float32)
    cs = jnp.concatenate([jnp.zeros((B, 1, G, C), jnp.float32), jnp.cumsum(pg, axis=1)], axis=1)
    t = jnp.arange(S)
    win = jnp.array(POOL_WINDOWS, dtype=jnp.int32)
    lo = jnp.maximum(t[:, None] + 1 - win[None, :], 0)
    window_sum = cs[:, 1:] - cs[:, lo, jnp.arange(G)[None, :]]
    count = (t[:, None] + 1 - lo).astype(jnp.float32)
    y = (window_sum / count[None, :, :, None] - pg).astype(p.dtype)
    y = jnp.einsum("bsgc,gcd->bsgd", y, w_pool) * s_pool.reshape(G, C)
    return y.reshape(B, S, G * C).astype(p.dtype)


def mlstm(xqk, xv, xo, ipre, fpre, w_mconv, b_mconv, w_mq, w_mk, b_i, b_f, g_mh):
    B, S, _ = xqk.shape
    H, Dh, L = M_HEADS, M_HEAD_DIM, M_CHUNK
    nc = S // L
    c = jax.nn.silu(causal_dwconv(xqk, w_mconv, b_mconv)).reshape(B, S, H, Dh)
    q = jnp.einsum("bshd,hde->bshe", c, w_mq).astype(jnp.float32)
    k = (jnp.einsum("bshd,hde->bshe", c, w_mk) * (Dh ** -0.5)).astype(jnp.float32)
    v = xv.reshape(B, S, H, Dh).astype(jnp.float32)
    log_i = (ipre + b_i).astype(jnp.float32)
    log_f = jax.nn.log_sigmoid((fpre + b_f).astype(jnp.float32))

    def to_chunks4(a):
        return a.reshape(B, nc, L, H, Dh).transpose(1, 0, 3, 2, 4)

    def to_chunks3(a):
        return a.reshape(B, nc, L, H).transpose(1, 0, 3, 2)

    tri = jnp.tril(jnp.ones((L, L), dtype=bool))

    def step(carry, inp):
        Cm, n, m = carry
        qc, kc, vc, li, lf = inp
        b = jnp.cumsum(lf, axis=-1)
        D = jnp.where(tri, b[..., :, None] - b[..., None, :] + li[..., None, :], -jnp.inf)
        inter = b + m[..., None]
        m_t = jnp.maximum(inter, jnp.max(D, axis=-1))
        g = jnp.exp(inter - m_t)
        sc = jnp.einsum("bhtd,bhsd->bhts", qc, kc) * jnp.exp(D - m_t[..., None])
        num = g[..., None] * jnp.einsum("bhtd,bhde->bhte", qc, Cm) + jnp.einsum("bhts,bhse->bhte", sc, vc)
        den = g * jnp.einsum("bhtd,bhd->bht", qc, n) + jnp.sum(sc, axis=-1)
        h = num / jnp.maximum(jnp.abs(den), jnp.exp(-m_t))[..., None]
        bL = b[..., -1]
        w_log = bL[..., None] - b + li
        m_new = jnp.maximum(bL + m, jnp.max(w_log, axis=-1))
        w = jnp.exp(w_log - m_new[..., None])
        decay = jnp.exp(bL + m - m_new)
        C_new = decay[..., None, None] * Cm + jnp.einsum("bhs,bhsd,bhse->bhde", w, kc, vc)
        n_new = decay[..., None] * n + jnp.einsum("bhs,bhsd->bhd", w, kc)
        return (C_new, n_new, m_new), h

    init = (jnp.zeros((B, H, Dh, Dh), jnp.float32), jnp.zeros((B, H, Dh), jnp.float32),
            jnp.zeros((B, H), jnp.float32))
    _, hs = lax.scan(step, init, (to_chunks4(q), to_chunks4(k), to_chunks4(v),
                                  to_chunks3(log_i), to_chunks3(log_f)))
    h = hs.transpose(1, 0, 3, 2, 4).reshape(B, S, H, Dh)
    mu = jnp.mean(h, axis=-1, keepdims=True)
    var = jnp.mean(jnp.square(h - mu), axis=-1, keepdims=True)
    h = (h - mu) * lax.rsqrt(var + LN_EPS) * g_mh.reshape(H, Dh)
    out = jax.nn.sigmoid(xo.astype(jnp.float32)).reshape(B, S, H, Dh) * h
    return out.reshape(B, S, M_W).astype(xqk.dtype)


def conv_glu_ffn(x, w_up, w_fconv, b_fconv, w_down):
    a, g = jnp.split(x @ w_up, 2, axis=-1)
    g = causal_dwconv(g, w_fconv, b_fconv)
    return (jax.nn.silu(g) * a) @ w_down


def setup_inputs(seed: int = 0) -> dict:
    key = jax.random.key(seed)
    ks = jax.random.split(key, 32)
    f32 = jnp.float32

    def nrm(k, shape, scale):
        return jax.random.normal(k, shape, f32) * scale

    def gain(k, shape):
        return 1.0 + 0.02 * jax.random.normal(k, shape, f32)

    L = DEPTH
    return {
        "x": nrm(ks[0], (BATCH, SEQ, D_MODEL), 1.0),
        "ln0_g": gain(ks[1], (D_MODEL,)),
        "ln0_b": nrm(ks[2], (D_MODEL,), 0.02),
        "w_in": nrm(ks[3], (L, D_MODEL, D_IN), D_MODEL ** -0.5),
        "g_qn": gain(ks[4], (L, Q_LORA)),
        "w_uq": nrm(ks[5], (L, Q_LORA, MLA_HEADS * (MLA_NOPE + MLA_ROPE)), Q_LORA ** -0.5),
        "g_kvn": gain(ks[6], (L, KV_LORA)),
        "w_ukv": nrm(ks[7], (L, KV_LORA, MLA_HEADS * (MLA_NOPE + MLA_V)), KV_LORA ** -0.5),
        "w_pool": nrm(ks[8], (L, POOL_GROUPS, POOL_CH, POOL_CH), POOL_CH ** -0.5),
        "s_pool": gain(ks[9], (L, POOL_W)),
        "w_mconv": nrm(ks[10], (L, M_CONV, M_W), M_CONV ** -0.5),
        "b_mconv": nrm(ks[11], (L, M_W), 0.02),
        "w_mq": nrm(ks[12], (L, M_HEADS, M_HEAD_DIM, M_HEAD_DIM), M_HEAD_DIM ** -0.5),
        "w_mk": nrm(ks[13], (L, M_HEADS, M_HEAD_DIM, M_HEAD_DIM), M_HEAD_DIM ** -0.5),
        "b_i": nrm(ks[14], (L, M_HEADS), 0.1),
        "b_f": jnp.linspace(3.0, 6.0, M_HEADS, dtype=f32)[None, :] + nrm(ks[15], (L, M_HEADS), 0.1),
        "g_mh": gain(ks[16], (L, M_W)),
        "w_out": nrm(ks[17], (L, D_MIX, D_MODEL), BETA * D_MIX ** -0.5),
        "ln1_g": gain(ks[18], (L, D_MODEL)),
        "ln1_b": nrm(ks[19], (L, D_MODEL), 0.02),
        "w_up": nrm(ks[20], (L, D_MODEL, 2 * D_FF), D_MODEL ** -0.5),
        "w_fconv": nrm(ks[21], (L, FFN_CONV, D_FF), FFN_CONV ** -0.5),
        "b_fconv": nrm(ks[22], (L, D_FF), 0.02),
        "w_down": nrm(ks[23], (L, D_FF, D_MODEL), BETA * D_FF ** -0.5),
        "ln2_g": gain(ks[24], (L, D_MODEL)),
        "ln2_b": nrm(ks[25], (L, D_MODEL), 0.02),
    }


def reference(x, ln0_g, ln0_b, w_in, g_qn, w_uq, g_kvn, w_ukv, w_pool, s_pool, w_mconv,
              b_mconv, w_mq, w_mk, b_i, b_f, g_mh, w_out, ln1_g, ln1_b, w_up, w_fconv,
              b_fconv, w_down, ln2_g, ln2_b):
    S = x.shape[1]
    cos, sin = rotary_tables(S, MLA_ROPE)
    h = layer_norm(x, ln0_g, ln0_b)
    for l in range(DEPTH):
        proj = h @ w_in[l]
        cq, ckv, kr, xp, xqk, xv, xo, ipre, fpre = jnp.split(proj, _split_points(), axis=-1)
        a = mla(cq, ckv, kr, g_qn[l], w_uq[l], g_kvn[l], w_ukv[l], cos, sin)
        p = multiscale_pool(xp, w_pool[l], s_pool[l])
        m = mlstm(xqk, xv, xo, ipre, fpre, w_mconv[l], b_mconv[l], w_mq[l], w_mk[l],
                  b_i[l], b_f[l], g_mh[l])
        mix = jnp.concatenate([a.astype(h.dtype), p.astype(h.dtype), m.astype(h.dtype)], axis=-1) @ w_out[l]
        h = layer_norm(ALPHA * h + mix, ln1_g[l], ln1_b[l])
        f = conv_glu_ffn(h, w_up[l], w_fconv[l], b_fconv[l], w_down[l])
        h = layer_norm(ALPHA * h + f, ln2_g[l], ln2_b[l])
    return h
```

```python
import functools

import jax
import jax.numpy as jnp
import numpy as np
from jax import lax
from jax.experimental import pallas as pl
from jax.experimental.pallas import tpu as pltpu

D_MODEL = 1024
DEPTH = 4
MLA_HEADS = 4
MLA_NOPE = 128
MLA_ROPE = 64
MLA_V = 128
Q_LORA = 384
KV_LORA = 256
ROPE_BASE = 10000.0
POOL_WINDOWS = (2, 4, 8, 16)
POOL_CH = 64
POOL_W = 256
M_HEADS = 4
M_HEAD_DIM = 64
M_W = 256
M_CONV = 4
D_FF = 2816
FFN_CONV = 3
LN_EPS = 1e-5
RMS_EPS = 1e-6
ALPHA = (2 * DEPTH) ** 0.25
ATT_SCALE = (MLA_NOPE + MLA_ROPE) ** -0.5

LANES = 128
QK_W = 2 * LANES
POOL_HALO = 16
CONV_HALO = 8
VMEM_LIMIT_BYTES = 56 * 1024 * 1024

PROJ_TM = 512
MIX_L = 256
ATT_T = 512
OUT_TM = 512
FFN_TM = 512
FFN_TF = 1408

BF16 = jnp.bfloat16
F32 = jnp.float32


def _cparams(sem):
    return pltpu.CompilerParams(dimension_semantics=sem, vmem_limit_bytes=VMEM_LIMIT_BYTES)


def _layer_norm_rows(x, g, b):
    mu = jnp.mean(x, axis=-1, keepdims=True)
    d = x - mu
    var = jnp.mean(d * d, axis=-1, keepdims=True)
    return d * lax.rsqrt(var + LN_EPS) * g + b


def _ln_kernel(x_ref, g_ref, b_ref, o_ref):
    o_ref[...] = _layer_norm_rows(x_ref[...], g_ref[...], b_ref[...])


def _input_ln(x2, g, b, tm):
    T, D = x2.shape
    return pl.pallas_call(
        _ln_kernel,
        grid=(T // tm,),
        in_specs=[pl.BlockSpec((tm, D), lambda i: (i, 0)),
                  pl.BlockSpec((1, D), lambda i: (0, 0)),
                  pl.BlockSpec((1, D), lambda i: (0, 0))],
        out_specs=pl.BlockSpec((tm, D), lambda i: (i, 0)),
        out_shape=jax.ShapeDtypeStruct((T, D), F32),
        compiler_params=_cparams(("arbitrary",)),
        name="input_ln",
    )(x2, g.reshape(1, D), b.reshape(1, D))


def _proj_kernel(h_ref, ct_ref, st_ref, wa_ref, wr_ref, wgt_ref, gq_ref, gkv_ref,
                 wq_ref, wkv_ref,
                 q_ref, k_ref, v_ref, rest_ref, gatest_ref):
    hb = h_ref[...].astype(BF16)
    pa = jnp.dot(hb, wa_ref[...], preferred_element_type=F32)
    rest_ref[...] = jnp.dot(hb, wr_ref[...], preferred_element_type=F32)
    gatest_ref[...] = lax.dot_general(wgt_ref[...], hb, (((1,), (1,)), ((), ())),
                                      preferred_element_type=F32)
    ct = ct_ref[...]
    st = st_ref[...]

    cq = pa[:, :Q_LORA]
    nq = cq * lax.rsqrt(jnp.mean(cq * cq, axis=-1, keepdims=True) + RMS_EPS) * gq_ref[...]
    yq = jnp.dot(nq.astype(BF16), wq_ref[...], preferred_element_type=F32)
    for hd in range(MLA_HEADS):
        base = hd * QK_W
        q_ref[:, base:base + LANES] = (yq[:, base:base + LANES] * ATT_SCALE).astype(BF16)
        rot = (yq[:, base + LANES:base + QK_W] * ct
               + yq[:, MLA_HEADS * QK_W + hd * LANES:MLA_HEADS * QK_W + (hd + 1) * LANES] * st)
        q_ref[:, base + LANES:base + QK_W] = (rot * ATT_SCALE).astype(BF16)

    ckv = pa[:, Q_LORA:Q_LORA + KV_LORA]
    nkv = ckv * lax.rsqrt(jnp.mean(ckv * ckv, axis=-1, keepdims=True) + RMS_EPS) * gkv_ref[...]
    ykv = jnp.dot(nkv.astype(BF16), wkv_ref[...], preferred_element_type=F32)
    o = Q_LORA + KV_LORA
    kpe = (pa[:, o:o + LANES] * ct + pa[:, o + LANES:o + 2 * LANES] * st).astype(BF16)
    for hd in range(MLA_HEADS):
        base = hd * QK_W
        k_ref[:, base:base + LANES] = ykv[:, hd * LANES:(hd + 1) * LANES].astype(BF16)
        k_ref[:, base + LANES:base + QK_W] = kpe
    v_ref[...] = ykv[:, MLA_HEADS * LANES:].astype(BF16)


def _proj(h, ctab, stab, lw, S, tm):
    T = h.shape[0]
    nsb = S // tm
    full = lambda a: pl.BlockSpec(a.shape, lambda i: (0,) * a.ndim)
    weights = [lw["wa"], lw["wr"], lw["wgt"], lw["gq"], lw["gkv"], lw["wq"], lw["wkv"]]
    HW = MLA_HEADS * QK_W
    return pl.pallas_call(
        _proj_kernel,
        grid=(T // tm,),
        in_specs=[pl.BlockSpec((tm, D_MODEL), lambda i: (i, 0)),
                  pl.BlockSpec((tm, LANES), lambda i: (i % nsb, 0)),
                  pl.BlockSpec((tm, LANES), lambda i: (i % nsb, 0))] + [full(w) for w in weights],
        out_specs=[pl.BlockSpec((tm, HW), lambda i: (i, 0)),
                   pl.BlockSpec((tm, HW), lambda i: (i, 0)),
                   pl.BlockSpec((tm, MLA_HEADS * MLA_V), lambda i: (i, 0)),
                   pl.BlockSpec((tm, 4 * 256), lambda i: (i, 0)),
                   pl.BlockSpec((16, tm), lambda i: (0, i))],
        out_shape=[jax.ShapeDtypeStruct((T, HW), BF16),
                   jax.ShapeDtypeStruct((T, HW), BF16),
                   jax.ShapeDtypeStruct((T, MLA_HEADS * MLA_V), BF16),
                   jax.ShapeDtypeStruct((T, 4 * 256), F32),
                   jax.ShapeDtypeStruct((16, T), F32)],
        compiler_params=_cparams(("arbitrary",)),
        name="proj",
    )(h, ctab, stab, *weights)


def _split3(x):
    hi = x.astype(BF16)
    r = x - hi.astype(F32)
    mid = r.astype(BF16)
    lo = (r - mid.astype(F32)).astype(BF16)
    return hi, mid, lo


def _mix_kernel(rest_ref, gt_ref, wpool_ref, spool_ref, wconv_ref, bconv_ref, wmq_ref, wmk_ref,
                gbias_ref, gmh_ref, o_ref,
                xp_buf, xc_buf, c_st, n_st, m_st, *, L):
    c = pl.program_id(1)
    H = M_HEADS
    HL = H * L

    @pl.when(c == 0)
    def _():
        xp_buf[0:POOL_HALO, :] = jnp.zeros((POOL_HALO, POOL_W), F32)
        xc_buf[0:CONV_HALO, :] = jnp.zeros((CONV_HALO, M_W), F32)
        c_st[...] = jnp.zeros_like(c_st)
        n_st[...] = jnp.zeros_like(n_st)
        m_st[...] = jnp.zeros_like(m_st)

    xp = rest_ref[:, 0:POOL_W]
    xp_buf[POOL_HALO:POOL_HALO + L, :] = xp
    lane = lax.broadcasted_iota(jnp.int32, (L, POOL_W), 1)
    grp = lane // POOL_CH
    acc = xp
    wsum = jnp.zeros((L, POOL_W), F32)
    for k in range(1, POOL_WINDOWS[-1]):
        acc = acc + xp_buf[POOL_HALO - k:POOL_HALO - k + L, :]
        if (k + 1) in POOL_WINDOWS:
            wsum = jnp.where(grp == POOL_WINDOWS.index(k + 1), acc, wsum)
    win = jnp.where(grp == 0, POOL_WINDOWS[0],
                    jnp.where(grp == 1, POOL_WINDOWS[1],
                              jnp.where(grp == 2, POOL_WINDOWS[2], POOL_WINDOWS[3])))
    tpos = c * L + lax.broadcasted_iota(jnp.int32, (L, POOL_W), 0)
    cnt = jnp.minimum(tpos + 1, win).astype(F32)
    y = wsum / cnt - xp
    pooled = jnp.dot(y.astype(BF16), wpool_ref[...], preferred_element_type=F32) * spool_ref[...]
    o_ref[:, 0:POOL_W] = pooled.astype(BF16)
    xp_buf[0:POOL_HALO, :] = xp[L - POOL_HALO:, :]

    xqk = rest_ref[:, POOL_W:POOL_W + M_W]
    xc_buf[CONV_HALO:CONV_HALO + L, :] = xqk
    conv = xqk * wconv_ref[M_CONV - 1:M_CONV, :] + bconv_ref[...]
    for k in range(1, M_CONV):
        conv = conv + xc_buf[CONV_HALO - k:CONV_HALO - k + L, :] * wconv_ref[M_CONV - 1 - k:M_CONV - k, :]
    xc_buf[0:CONV_HALO, :] = xqk[L - CONV_HALO:, :]
    cs = (conv * jax.nn.sigmoid(conv)).astype(BF16)
    q = jnp.dot(cs, wmq_ref[...], preferred_element_type=F32)
    kk = jnp.dot(cs, wmk_ref[...], preferred_element_type=F32) * (M_HEAD_DIM ** -0.5)
    v = rest_ref[:, POOL_W + M_W:POOL_W + 2 * M_W]
    xo = rest_ref[:, POOL_W + 2 * M_W:POOL_W + 3 * M_W]
    kb = kk.astype(BF16)
    vb = v.astype(BF16)

    gt = gt_ref[...] + gbias_ref[...]
    li_r = gt
    lf_r = jax.nn.log_sigmoid(gt)
    ri = lax.broadcasted_iota(jnp.int32, (L, L), 0)
    ci = lax.broadcasted_iota(jnp.int32, (L, L), 1)
    upper = (ri <= ci).astype(BF16)
    hi, mid, lo = _split3(lf_r)
    b_r = (jnp.dot(hi, upper, preferred_element_type=F32)
           + jnp.dot(mid, upper, preferred_element_type=F32)
           + jnp.dot(lo, upper, preferred_element_type=F32))

    def stack_rows(x, off):
        return jnp.concatenate([jnp.broadcast_to(x[off + hh:off + hh + 1, :], (L, L)) for hh in range(H)], axis=0)

    LF = stack_rows(lf_r, 4)
    LI = stack_rows(li_r, 0)
    A = LI - stack_rows(b_r, 4)
    trow = lax.broadcasted_iota(jnp.int32, (HL, L), 0) % L
    scol = lax.broadcasted_iota(jnp.int32, (HL, L), 1)
    causal = scol <= trow
    diag = scol == trow
    b_c = jnp.sum(jnp.where(causal, LF, 0.0), axis=-1, keepdims=True)
    a_c = jnp.sum(jnp.where(diag, A, 0.0), axis=-1, keepdims=True)
    cm_c = jnp.max(jnp.where(causal, A, -jnp.inf), axis=-1, keepdims=True)
    m_prev = m_st[...]
    M_c = jnp.maximum(m_prev, cm_c)
    E = jnp.exp(jnp.where(causal, A - M_c, -jnp.inf))
    g_c = jnp.exp(m_prev - M_c)
    mt_c = b_c + M_c
    eneg_c = jnp.exp(-mt_c)

    def last_row_bcast(x):
        return jnp.concatenate([jnp.broadcast_to(x[(hh + 1) * L - 1:(hh + 1) * L, :], (L, 1)) for hh in range(H)], axis=0)

    bL_c = last_row_bcast(b_c)
    m_new_c = last_row_bcast(mt_c)
    w_c = jnp.exp(bL_c + a_c - m_new_c)
    decay_c = jnp.exp(bL_c + m_prev - m_new_c)

    hrow = lax.broadcasted_iota(jnp.int32, (HL, M_W), 0) // L
    hlane = lax.broadcasted_iota(jnp.int32, (HL, M_W), 1) // M_HEAD_DIM
    hmask = hrow == hlane
    q4 = jnp.where(hmask, jnp.concatenate([q] * H, axis=0), 0.0)
    q4b = q4.astype(BF16)
    qk = lax.dot_general(q4b, kb, (((1,), (1,)), ((), ())), preferred_element_type=F32)
    sc = qk * E
    den_intra = jnp.sum(sc, axis=-1, keepdims=True)
    num_intra = jnp.dot(sc.astype(BF16), vb, preferred_element_type=F32)
    c_prev = c_st[...]
    inter = jnp.dot(q4b, c_prev.astype(BF16), preferred_element_type=F32)
    qn = jnp.sum(q4 * n_st[...], axis=-1, keepdims=True)
    num = g_c * inter + num_intra
    den = g_c * qn + den_intra
    hst = num / jnp.maximum(jnp.abs(den), eneg_c)
    mu = jnp.sum(jnp.where(hmask, hst, 0.0), axis=-1, keepdims=True) * (1.0 / M_HEAD_DIM)
    dd = jnp.where(hmask, hst - mu, 0.0)
    var = jnp.sum(dd * dd, axis=-1, keepdims=True) * (1.0 / M_HEAD_DIM)
    yn = dd * lax.rsqrt(var + LN_EPS)
    ync = yn[0:L] + yn[L:2 * L] + yn[2 * L:3 * L] + yn[3 * L:4 * L]
    o_ref[:, POOL_W:POOL_W + M_W] = (jax.nn.sigmoid(xo) * (ync * gmh_ref[...])).astype(BF16)

    kw4 = jnp.where(hmask, jnp.concatenate([kk] * H, axis=0), 0.0) * w_c
    v4b = jnp.concatenate([vb] * H, axis=0)
    upd = lax.dot_general(kw4.astype(BF16), v4b, (((0,), (0,)), ((), ())), preferred_element_type=F32)
    dec_col = jnp.concatenate([decay_c[hh * L:hh * L + M_HEAD_DIM, :] for hh in range(H)], axis=0)
    bd = (lax.broadcasted_iota(jnp.int32, (M_W, M_W), 0) // M_HEAD_DIM
          == lax.broadcasted_iota(jnp.int32, (M_W, M_W), 1) // M_HEAD_DIM)
    c_st[...] = jnp.where(bd, dec_col * c_prev + upd, 0.0)
    lane_h = lax.broadcasted_iota(jnp.int32, (1, M_W), 1) // M_HEAD_DIM
    dec_row = jnp.zeros((1, M_W), F32)
    for hh in range(H):
        dec_row = jnp.where(lane_h == hh, jnp.broadcast_to(decay_c[hh * L:hh * L + 1, :], (1, M_W)), dec_row)
    n_st[...] = dec_row * n_st[...] + jnp.sum(kw4, axis=0, keepdims=True)
    m_st[...] = m_new_c


def _mix(rest, gatest, lw, B, S, L):
    T = rest.shape[0]
    nc = S // L
    full = lambda a: pl.BlockSpec(a.shape, lambda b, c: (0,) * a.ndim)
    weights = [lw["wpool"], lw["spool"], lw["wconv"], lw["bconv"], lw["wmq"], lw["wmk"], lw["gbias"], lw["gmh"]]
    return pl.pallas_call(
        functools.partial(_mix_kernel, L=L),
        grid=(B, nc),
        in_specs=[pl.BlockSpec((L, 4 * 256), lambda b, c: (b * nc + c, 0)),
                  pl.BlockSpec((16, L), lambda b, c: (0, b * nc + c))] + [full(w) for w in weights],
        out_specs=pl.BlockSpec((L, POOL_W + M_W), lambda b, c: (b * nc + c, 0)),
        out_shape=jax.ShapeDtypeStruct((T, POOL_W + M_W), BF16),
        scratch_shapes=[pltpu.VMEM((POOL_HALO + L, POOL_W), F32),
                        pltpu.VMEM((CONV_HALO + L, M_W), F32),
                        pltpu.VMEM((M_W, M_W), F32),
                        pltpu.VMEM((1, M_W), F32),
                        pltpu.VMEM((M_HEADS * L, 1), F32)],
        compiler_params=_cparams(("arbitrary", "arbitrary")),
        name="pool_mlstm",
    )(rest, gatest, *weights)


def _attn_kernel(q_ref, k_ref, v_ref, o_ref, m_sc, l_sc, acc_sc, *, t):
    qi = pl.program_id(2)
    q = q_ref[...]
    m_sc[...] = jnp.full_like(m_sc, -jnp.inf)
    l_sc[...] = jnp.zeros_like(l_sc)
    acc_sc[...] = jnp.zeros_like(acc_sc)

    def step(j, masked):
        start = pl.multiple_of(j * t, t)
        k = k_ref[pl.ds(start, t), :]
        v = v_ref[pl.ds(start, t), :]
        s = lax.dot_general(q, k, (((1,), (1,)), ((), ())), preferred_element_type=F32)
        if masked:
            ri = lax.broadcasted_iota(jnp.int32, (t, t), 0)
            ci = lax.broadcasted_iota(jnp.int32, (t, t), 1)
            s = jnp.where(ci <= ri, s, -jnp.inf)
        m_old = m_sc[...]
        m_new = jnp.maximum(m_old, jnp.max(s, axis=-1, keepdims=True))
        p = jnp.exp(s - m_new)
        alpha = jnp.exp(m_old - m_new)
        l_sc[...] = alpha * l_sc[...] + jnp.sum(p, axis=-1, keepdims=True)
        acc_sc[...] = alpha * acc_sc[...] + jnp.dot(p.astype(BF16), v, preferred_element_type=F32)
        m_sc[...] = m_new

    def body(j, carry):
        step(j, False)
        return carry

    lax.fori_loop(0, qi, body, 0)
    step(qi, True)
    o_ref[...] = (acc_sc[...] / l_sc[...]).astype(o_ref.dtype)


def _attention(qc, kc, vv, B, S, t):
    T = qc.shape[0]
    nq = S // t
    return pl.pallas_call(
        functools.partial(_attn_kernel, t=t),
        grid=(B, MLA_HEADS, nq),
        in_specs=[pl.BlockSpec((t, QK_W), lambda b, h, i: (b * nq + i, h)),
                  pl.BlockSpec((S, QK_W), lambda b, h, i: (b, h)),
                  pl.BlockSpec((S, MLA_V), lambda b, h, i: (b, h))],
        out_specs=pl.BlockSpec((t, MLA_V), lambda b, h, i: (b * nq + i, h)),
        out_shape=jax.ShapeDtypeStruct((T, MLA_HEADS * MLA_V), BF16),
        scratch_shapes=[pltpu.VMEM((t, 1), F32), pltpu.VMEM((t, 1), F32), pltpu.VMEM((t, MLA_V), F32)],
        compiler_params=_cparams(("arbitrary", "arbitrary", "arbitrary")),
        name="mla_attention",
    )(qc, kc, vv)


def _outproj_kernel(h_ref, a_ref, pm_ref, woa_ref, wopm_ref, g_ref, b_ref, o_ref):
    mix = (jnp.dot(a_ref[...], woa_ref[...], preferred_element_type=F32)
           + jnp.dot(pm_ref[...], wopm_ref[...], preferred_element_type=F32))
    o_ref[...] = _layer_norm_rows(ALPHA * h_ref[...] + mix, g_ref[...], b_ref[...])


def _outproj(h, a, pm, lw, tm):
    T = h.shape[0]
    full = lambda x: pl.BlockSpec(x.shape, lambda i: (0,) * x.ndim)
    weights = [lw["woa"], lw["wopm"], lw["ln1g"], lw["ln1b"]]
    return pl.pallas_call(
        _outproj_kernel,
        grid=(T // tm,),
        in_specs=[pl.BlockSpec((tm, D_MODEL), lambda i: (i, 0)),
                  pl.BlockSpec((tm, a.shape[1]), lambda i: (i, 0)),
                  pl.BlockSpec((tm, pm.shape[1]), lambda i: (i, 0))] + [full(w) for w in weights],
        out_specs=pl.BlockSpec((tm, D_MODEL), lambda i: (i, 0)),
        out_shape=jax.ShapeDtypeStruct((T, D_MODEL), F32),
        compiler_params=_cparams(("arbitrary",)),
        name="outproj_ln",
    )(h, a, pm, *weights)


def _ffn_kernel(h_ref, wa_ref, wg_ref, wfc_ref, bfc_ref, wd_ref, g_ref, b_ref, o_ref,
                acc_sc, g_buf, tail_sc, *, tm, nsb):
    i = pl.program_id(0)
    j = pl.program_id(1)
    nj = pl.num_programs(1)
    hb = h_ref[...].astype(BF16)
    a = jnp.dot(hb, wa_ref[...], preferred_element_type=F32)
    g_buf[CONV_HALO:CONV_HALO + tm, :] = jnp.dot(hb, wg_ref[...], preferred_element_type=F32)
    seq_start = (i % nsb) == 0

    @pl.when(seq_start)
    def _():
        g_buf[0:CONV_HALO, :] = jnp.zeros((CONV_HALO, g_buf.shape[1]), F32)

    @pl.when(jnp.logical_not(seq_start))
    def _():
        g_buf[0:CONV_HALO, :] = tail_sc[j]

    conv = g_buf[CONV_HALO:CONV_HALO + tm, :] * wfc_ref[FFN_CONV - 1:FFN_CONV, :] + bfc_ref[...]
    for k in range(1, FFN_CONV):
        conv = conv + g_buf[CONV_HALO - k:CONV_HALO - k + tm, :] * wfc_ref[FFN_CONV - 1 - k:FFN_CONV - k, :]
    tail_sc[j] = g_buf[tm:tm + CONV_HALO, :]
    u = (conv * jax.nn.sigmoid(conv) * a).astype(BF16)
    part = jnp.dot(u, wd_ref[...], preferred_element_type=F32)

    @pl.when(j == 0)
    def _():
        acc_sc[...] = part

    @pl.when(j > 0)
    def _():
        acc_sc[...] += part

    @pl.when(j == nj - 1)
    def _():
        o_ref[...] = _layer_norm_rows(ALPHA * h_ref[...] + acc_sc[...], g_ref[...], b_ref[...])


def _ffn(h, lw, S, tm, tf):
    T = h.shape[0]
    nj = D_FF // tf
    nsb = S // tm
    return pl.pallas_call(
        functools.partial(_ffn_kernel, tm=tm, nsb=nsb),
        grid=(T // tm, nj),
        in_specs=[pl.BlockSpec((tm, D_MODEL), lambda i, j: (i, 0)),
                  pl.BlockSpec((D_MODEL, tf), lambda i, j: (0, j)),
                  pl.BlockSpec((D_MODEL, tf), lambda i, j: (0, j)),
                  pl.BlockSpec((FFN_CONV, tf), lambda i, j: (0, j)),
                  pl.BlockSpec((1, tf), lambda i, j: (0, j)),
                  pl.BlockSpec((tf, D_MODEL), lambda i, j: (j, 0)),
                  pl.BlockSpec((1, D_MODEL), lambda i, j: (0, 0)),
                  pl.BlockSpec((1, D_MODEL), lambda i, j: (0, 0))],
        out_specs=pl.BlockSpec((tm, D_MODEL), lambda i, j: (i, 0)),
        out_shape=jax.ShapeDtypeStruct((T, D_MODEL), F32),
        scratch_shapes=[pltpu.VMEM((tm, D_MODEL), F32),
                        pltpu.VMEM((CONV_HALO + tm, tf), F32),
                        pltpu.VMEM((nj, CONV_HALO, tf), F32)],
        compiler_params=_cparams(("arbitrary", "arbitrary")),
        name="ffn_ln",
    )(h, lw["wup_a"], lw["wup_g"], lw["wfc"], lw["bfc"], lw["wd"], lw["ln2g"], lw["ln2b"])


def _block_diag(w):
    G, C, _ = w.shape
    out = jnp.zeros((G * C, G * C), w.dtype)
    for g in range(G):
        out = out.at[g * C:(g + 1) * C, g * C:(g + 1) * C].set(w[g])
    return out


def _prep_layer(l, w_in, g_qn, w_uq, g_kvn, w_ukv, w_pool, s_pool, w_mconv, b_mconv, w_mq, w_mk,
                b_i, b_f, g_mh, w_out, ln1_g, ln1_b, w_up, w_fconv, b_fconv, w_down, ln2_g, ln2_b):
    half = MLA_ROPE // 2
    wi = w_in[l]
    o_kr = Q_LORA + KV_LORA
    o_rest = o_kr + MLA_ROPE
    o_gate = o_rest + 4 * 256
    kr = wi[:, o_kr:o_kr + MLA_ROPE]
    zpad = jnp.zeros((D_MODEL, LANES - MLA_ROPE), F32)
    kr_sw = jnp.concatenate([-kr[:, half:], kr[:, :half]], axis=1)
    wa = jnp.concatenate([wi[:, :o_kr], kr, zpad, kr_sw, zpad], axis=1)
    wr = wi[:, o_rest:o_gate]
    wgate = wi[:, o_gate:o_gate + 2 * M_HEADS]
    wgt = jnp.concatenate([wgate.T, jnp.zeros((16 - 2 * M_HEADS, D_MODEL), F32)], axis=0)

    wuq = w_uq[l].reshape(Q_LORA, MLA_HEADS, MLA_NOPE + MLA_ROPE)
    zq = jnp.zeros((Q_LORA, MLA_HEADS, LANES - MLA_ROPE), F32)
    w1 = jnp.concatenate([wuq, zq], axis=2).reshape(Q_LORA, MLA_HEADS * QK_W)
    rope = wuq[:, :, MLA_NOPE:]
    w2 = jnp.concatenate([-rope[:, :, half:], rope[:, :, :half], zq], axis=2).reshape(Q_LORA, MLA_HEADS * LANES)
    wq = jnp.concatenate([w1, w2], axis=1)

    wukv = w_ukv[l].reshape(KV_LORA, MLA_HEADS, MLA_NOPE + MLA_V)
    wkv = jnp.concatenate([wukv[:, :, :MLA_NOPE].reshape(KV_LORA, -1),
                           wukv[:, :, MLA_NOPE:].reshape(KV_LORA, -1)], axis=1)

    gbias = jnp.concatenate([b_i[l], b_f[l], jnp.zeros((16 - 2 * M_HEADS,), F32)]).reshape(16, 1)
    wo = w_out[l]
    return dict(
        wa=wa.astype(BF16), wr=wr.astype(BF16), wgt=wgt.astype(BF16),
        gq=g_qn[l].reshape(1, Q_LORA), gkv=g_kvn[l].reshape(1, KV_LORA),
        wq=wq.astype(BF16), wkv=wkv.astype(BF16),
        wpool=_block_diag(w_pool[l]).astype(BF16), spool=s_pool[l].reshape(1, POOL_W),
        wconv=w_mconv[l], bconv=b_mconv[l].reshape(1, M_W),
        wmq=_block_diag(w_mq[l]).astype(BF16), wmk=_block_diag(w_mk[l]).astype(BF16),
        gbias=gbias, gmh=g_mh[l].reshape(1, M_W),
        woa=wo[:MLA_HEADS * MLA_V].astype(BF16), wopm=wo[MLA_HEADS * MLA_V:].astype(BF16),
        ln1g=ln1_g[l].reshape(1, D_MODEL), ln1b=ln1_b[l].reshape(1, D_MODEL),
        wup_a=w_up[l][:, :D_FF].astype(BF16), wup_g=w_up[l][:, D_FF:].astype(BF16),
        wfc=w_fconv[l], bfc=b_fconv[l].reshape(1, D_FF), wd=w_down[l].astype(BF16),
        ln2g=ln2_g[l].reshape(1, D_MODEL), ln2b=ln2_b[l].reshape(1, D_MODEL),
    )


def _rope_tables(S):
    inv = 1.0 / (ROPE_BASE ** (jnp.arange(0, MLA_ROPE, 2, dtype=F32) / MLA_ROPE))
    ang = jnp.arange(S, dtype=F32)[:, None] * inv[None, :]
    z = jnp.zeros((S, LANES - MLA_ROPE), F32)
    ctab = jnp.concatenate([jnp.cos(ang), jnp.cos(ang), z], axis=1)
    stab = jnp.concatenate([jnp.sin(ang), jnp.sin(ang), z], axis=1)
    return ctab, stab


def _tile(pref, n):
    t = min(pref, n)
    assert n % t == 0, (pref, n)
    return t


def kernel(x, ln0_g, ln0_b, w_in, g_qn, w_uq, g_kvn, w_ukv, w_pool, s_pool, w_mconv, b_mconv, w_mq, w_mk, b_i, b_f, g_mh, w_out, ln1_g, ln1_b, w_up, w_fconv, b_fconv, w_down, ln2_g, ln2_b):
    B, S, D = x.shape
    assert D == D_MODEL
    T = B * S
    params = (w_in, g_qn, w_uq, g_kvn, w_ukv, w_pool, s_pool, w_mconv, b_mconv, w_mq, w_mk, b_i, b_f,
              g_mh, w_out, ln1_g, ln1_b, w_up, w_fconv, b_fconv, w_down, ln2_g, ln2_b)
    ctab, stab = _rope_tables(S)
    h = _input_ln(x.reshape(T, D), ln0_g, ln0_b, _tile(PROJ_TM, S))
    for l in range(DEPTH):
        lw = _prep_layer(l, *params)
        qc, kc, vv, rest, gatest = _proj(h, ctab, stab, lw, S, _tile(PROJ_TM, S))
        pm = _mix(rest, gatest, lw, B, S, _tile(MIX_L, S))
        a = _attention(qc, kc, vv, B, S, _tile(ATT_T, S))
        h1 = _outproj(h, a, pm, lw, _tile(OUT_TM, S))
        h = _ffn(h1, lw, S, _tile(FFN_TM, S), FFN_TF)
    return h.reshape(B, S, D)
```

```python
import functools

import jax
import jax.numpy as jnp
import numpy as np
from jax import lax
from jax.experimental import pallas as pl
from jax.experimental.pallas import tpu as pltpu

D_MODEL = 1024
DEPTH = 4
MLA_HEADS = 4
MLA_NOPE = 128
MLA_ROPE = 64
MLA_V = 128
Q_LORA = 384
KV_LORA = 256
ROPE_BASE = 10000.0
POOL_WINDOWS = (2, 4, 8, 16)
POOL_CH = 64
POOL_W = 256
M_HEADS = 4
M_HEAD_DIM = 64
M_W = 256
M_CONV = 4
D_FF = 2816
FFN_CONV = 3
LN_EPS = 1e-5
RMS_EPS = 1e-6
ALPHA = (2 * DEPTH) ** 0.25
ATT_SCALE = (MLA_NOPE + MLA_ROPE) ** -0.5
LOG2_E = 1.4426950408889634
Q_SCALE = ATT_SCALE * LOG2_E

LANES = 128
QK_W = 2 * LANES
POOL_HALO = 16
CONV_HALO = 8
VMEM_LIMIT_BYTES = 56 * 1024 * 1024

PROJ_TM = 512
MIX_L = 256
ATT_T = 512
ATT_NSUB = 2
FFN_TM = 512
FFN_NSUB = 2
FFN_TF = 256

BF16 = jnp.bfloat16
F32 = jnp.float32


def _cparams(sem):
    return pltpu.CompilerParams(dimension_semantics=sem, vmem_limit_bytes=VMEM_LIMIT_BYTES)


def _layer_norm_rows(x, g, b):
    mu = jnp.mean(x, axis=-1, keepdims=True)
    d = x - mu
    var = jnp.mean(d * d, axis=-1, keepdims=True)
    return d * lax.rsqrt(var + LN_EPS) * g + b


def _ln_kernel(x_ref, g_ref, b_ref, o_ref):
    o_ref[...] = _layer_norm_rows(x_ref[...], g_ref[...], b_ref[...])


def _input_ln(x2, g, b, tm):
    T, D = x2.shape
    return pl.pallas_call(
        _ln_kernel,
        grid=(T // tm,),
        in_specs=[pl.BlockSpec((tm, D), lambda i: (i, 0)),
                  pl.BlockSpec((1, D), lambda i: (0, 0)),
                  pl.BlockSpec((1, D), lambda i: (0, 0))],
        out_specs=pl.BlockSpec((tm, D), lambda i: (i, 0)),
        out_shape=jax.ShapeDtypeStruct((T, D), F32),
        compiler_params=_cparams(("arbitrary",)),
        name="input_ln",
    )(x2, g.reshape(1, D), b.reshape(1, D))


def _proj_kernel(h_ref, ct_ref, st_ref, wa_ref, wr_ref, wgt_ref, gq_ref, gkv_ref,
                 wq_ref, wkv_ref,
                 q_ref, k_ref, v_ref, rest_ref, gatest_ref):
    hb = h_ref[...].astype(BF16)
    pa = jnp.dot(hb, wa_ref[...], preferred_element_type=F32)
    rest_ref[...] = jnp.dot(hb, wr_ref[...], preferred_element_type=F32)
    gatest_ref[...] = lax.dot_general(wgt_ref[...], hb, (((1,), (1,)), ((), ())),
                                      preferred_element_type=F32)
    ct = ct_ref[...]
    st = st_ref[...]

    cq = pa[:, :Q_LORA]
    nq = cq * lax.rsqrt(jnp.mean(cq * cq, axis=-1, keepdims=True) + RMS_EPS) * gq_ref[...]
    yq = jnp.dot(nq.astype(BF16), wq_ref[...], preferred_element_type=F32)
    for hd in range(MLA_HEADS):
        base = hd * QK_W
        q_ref[:, base:base + LANES] = (yq[:, base:base + LANES] * Q_SCALE).astype(BF16)
        rot = (yq[:, base + LANES:base + QK_W] * ct
               + yq[:, MLA_HEADS * QK_W + hd * LANES:MLA_HEADS * QK_W + (hd + 1) * LANES] * st)
        q_ref[:, base + LANES:base + QK_W] = (rot * Q_SCALE).astype(BF16)

    ckv = pa[:, Q_LORA:Q_LORA + KV_LORA]
    nkv = ckv * lax.rsqrt(jnp.mean(ckv * ckv, axis=-1, keepdims=True) + RMS_EPS) * gkv_ref[...]
    ykv = jnp.dot(nkv.astype(BF16), wkv_ref[...], preferred_element_type=F32)
    o = Q_LORA + KV_LORA
    kpe = (pa[:, o:o + LANES] * ct + pa[:, o + LANES:o + 2 * LANES] * st).astype(BF16)
    ones = jnp.ones((pa.shape[0], MLA_V), BF16)
    for hd in range(MLA_HEADS):
        base = hd * QK_W
        k_ref[:, base:base + LANES] = ykv[:, hd * LANES:(hd + 1) * LANES].astype(BF16)
        k_ref[:, base + LANES:base + QK_W] = kpe
        v_ref[:, base:base + MLA_V] = ykv[:, (MLA_HEADS + hd) * LANES:(MLA_HEADS + hd + 1) * LANES].astype(BF16)
        v_ref[:, base + MLA_V:base + 2 * MLA_V] = ones


def _proj(h, ctab, stab, lw, S, tm):
    T = h.shape[0]
    nsb = S // tm
    full = lambda a: pl.BlockSpec(a.shape, lambda i: (0,) * a.ndim)
    weights = [lw["wa"], lw["wr"], lw["wgt"], lw["gq"], lw["gkv"], lw["wq"], lw["wkv"]]
    HW = MLA_HEADS * QK_W
    return pl.pallas_call(
        _proj_kernel,
        grid=(T // tm,),
        in_specs=[pl.BlockSpec((tm, D_MODEL), lambda i: (i, 0)),
                  pl.BlockSpec((tm, LANES), lambda i: (i % nsb, 0)),
                  pl.BlockSpec((tm, LANES), lambda i: (i % nsb, 0))] + [full(w) for w in weights],
        out_specs=[pl.BlockSpec((tm, HW), lambda i: (i, 0)),
                   pl.BlockSpec((tm, HW), lambda i: (i, 0)),
                   pl.BlockSpec((tm, HW), lambda i: (i, 0)),
                   pl.BlockSpec((tm, 4 * 256), lambda i: (i, 0)),
                   pl.BlockSpec((16, tm), lambda i: (0, i))],
        out_shape=[jax.ShapeDtypeStruct((T, HW), BF16),
                   jax.ShapeDtypeStruct((T, HW), BF16),
                   jax.ShapeDtypeStruct((T, HW), BF16),
                   jax.ShapeDtypeStruct((T, 4 * 256), F32),
                   jax.ShapeDtypeStruct((16, T), F32)],
        compiler_params=_cparams(("arbitrary",)),
        name="proj",
    )(h, ctab, stab, *weights)


def _split3(x):
    hi = x.astype(BF16)
    r = x - hi.astype(F32)
    mid = r.astype(BF16)
    lo = (r - mid.astype(F32)).astype(BF16)
    return hi, mid, lo


def _mix_kernel(rest_ref, gt_ref, wpool_ref, spool_ref, wconv_ref, bconv_ref, wmq_ref, wmk_ref,
                gbias_ref, gmh_ref, o_ref,
                xp_buf, xc_buf, c_st, n_st, m_st, *, L):
    c = pl.program_id(1)
    H = M_HEADS
    HL = H * L

    @pl.when(c == 0)
    def _():
        xp_buf[0:POOL_HALO, :] = jnp.zeros((POOL_HALO, POOL_W), F32)
        xc_buf[0:CONV_HALO, :] = jnp.zeros((CONV_HALO, M_W), F32)
        c_st[...] = jnp.zeros_like(c_st)
        n_st[...] = jnp.zeros_like(n_st)
        m_st[...] = jnp.zeros_like(m_st)

    xp = rest_ref[:, 0:POOL_W]
    xp_buf[POOL_HALO:POOL_HALO + L, :] = xp
    lane = lax.broadcasted_iota(jnp.int32, (L, POOL_W), 1)
    grp = lane // POOL_CH
    acc = xp
    wsum = jnp.zeros((L, POOL_W), F32)
    for k in range(1, POOL_WINDOWS[-1]):
        acc = acc + xp_buf[POOL_HALO - k:POOL_HALO - k + L, :]
        if (k + 1) in POOL_WINDOWS:
            wsum = jnp.where(grp == POOL_WINDOWS.index(k + 1), acc, wsum)
    win = jnp.where(grp == 0, POOL_WINDOWS[0],
                    jnp.where(grp == 1, POOL_WINDOWS[1],
                              jnp.where(grp == 2, POOL_WINDOWS[2], POOL_WINDOWS[3])))
    tpos = c * L + lax.broadcasted_iota(jnp.int32, (L, POOL_W), 0)
    cnt = jnp.minimum(tpos + 1, win).astype(F32)
    y = wsum / cnt - xp
    pooled = jnp.dot(y.astype(BF16), wpool_ref[...], preferred_element_type=F32) * spool_ref[...]
    o_ref[:, 0:POOL_W] = pooled.astype(BF16)
    xp_buf[0:POOL_HALO, :] = xp[L - POOL_HALO:, :]

    xqk = rest_ref[:, POOL_W:POOL_W + M_W]
    xc_buf[CONV_HALO:CONV_HALO + L, :] = xqk
    conv = xqk * wconv_ref[M_CONV - 1:M_CONV, :] + bconv_ref[...]
    for k in range(1, M_CONV):
        conv = conv + xc_buf[CONV_HALO - k:CONV_HALO - k + L, :] * wconv_ref[M_CONV - 1 - k:M_CONV - k, :]
    xc_buf[0:CONV_HALO, :] = xqk[L - CONV_HALO:, :]
    cs = (conv * jax.nn.sigmoid(conv)).astype(BF16)
    q = jnp.dot(cs, wmq_ref[...], preferred_element_type=F32)
    kk = jnp.dot(cs, wmk_ref[...], preferred_element_type=F32) * (M_HEAD_DIM ** -0.5)
    v = rest_ref[:, POOL_W + M_W:POOL_W + 2 * M_W]
    xo = rest_ref[:, POOL_W + 2 * M_W:POOL_W + 3 * M_W]
    kb = kk.astype(BF16)
    vb = v.astype(BF16)

    gt = gt_ref[...] + gbias_ref[...]
    li_r = gt
    lf_r = jax.nn.log_sigmoid(gt)
    ri = lax.broadcasted_iota(jnp.int32, (L, L), 0)
    ci = lax.broadcasted_iota(jnp.int32, (L, L), 1)
    upper = (ri <= ci).astype(BF16)
    hi, mid, lo = _split3(lf_r)
    b_r = (jnp.dot(hi, upper, preferred_element_type=F32)
           + jnp.dot(mid, upper, preferred_element_type=F32)
           + jnp.dot(lo, upper, preferred_element_type=F32))

    def stack_rows(x, off):
        return jnp.concatenate([jnp.broadcast_to(x[off + hh:off + hh + 1, :], (L, L)) for hh in range(H)], axis=0)

    LF = stack_rows(lf_r, 4)
    LI = stack_rows(li_r, 0)
    A = LI - stack_rows(b_r, 4)
    trow = lax.broadcasted_iota(jnp.int32, (HL, L), 0) % L
    scol = lax.broadcasted_iota(jnp.int32, (HL, L), 1)
    causal = scol <= trow
    diag = scol == trow
    b_c = jnp.sum(jnp.where(causal, LF, 0.0), axis=-1, keepdims=True)
    a_c = jnp.sum(jnp.where(diag, A, 0.0), axis=-1, keepdims=True)
    cm_c = jnp.max(jnp.where(causal, A, -jnp.inf), axis=-1, keepdims=True)
    m_prev = m_st[...]
    M_c = jnp.maximum(m_prev, cm_c)
    E = jnp.exp(jnp.where(causal, A - M_c, -jnp.inf))
    g_c = jnp.exp(m_prev - M_c)
    mt_c = b_c + M_c
    eneg_c = jnp.exp(-mt_c)

    def last_row_bcast(x):
        return jnp.concatenate([jnp.broadcast_to(x[(hh + 1) * L - 1:(hh + 1) * L, :], (L, 1)) for hh in range(H)], axis=0)

    bL_c = last_row_bcast(b_c)
    m_new_c = last_row_bcast(mt_c)
    w_c = jnp.exp(bL_c + a_c - m_new_c)
    decay_c = jnp.exp(bL_c + m_prev - m_new_c)

    hrow = lax.broadcasted_iota(jnp.int32, (HL, M_W), 0) // L
    hlane = lax.broadcasted_iota(jnp.int32, (HL, M_W), 1) // M_HEAD_DIM
    hmask = hrow == hlane
    q4 = jnp.where(hmask, jnp.concatenate([q] * H, axis=0), 0.0)
    q4b = q4.astype(BF16)
    qk = lax.dot_general(q4b, kb, (((1,), (1,)), ((), ())), preferred_element_type=F32)
    sc = qk * E
    den_intra = jnp.sum(sc, axis=-1, keepdims=True)
    num_intra = jnp.dot(sc.astype(BF16), vb, preferred_element_type=F32)
    c_prev = c_st[...]
    inter = jnp.dot(q4b, c_prev.astype(BF16), preferred_element_type=F32)
    qn = jnp.sum(q4 * n_st[...], axis=-1, keepdims=True)
    num = g_c * inter + num_intra
    den = g_c * qn + den_intra
    hst = num / jnp.maximum(jnp.abs(den), eneg_c)
    mu = jnp.sum(jnp.where(hmask, hst, 0.0), axis=-1, keepdims=True) * (1.0 / M_HEAD_DIM)
    dd = jnp.where(hmask, hst - mu, 0.0)
    var = jnp.sum(dd * dd, axis=-1, keepdims=True) * (1.0 / M_HEAD_DIM)
    yn = dd * lax.rsqrt(var + LN_EPS)
    ync = yn[0:L] + yn[L:2 * L] + yn[2 * L:3 * L] + yn[3 * L:4 * L]
    o_ref[:, POOL_W:POOL_W + M_W] = (jax.nn.sigmoid(xo) * (ync * gmh_ref[...])).astype(BF16)

    kw4 = jnp.where(hmask, jnp.concatenate([kk] * H, axis=0), 0.0) * w_c
    v4b = jnp.concatenate([vb] * H, axis=0)
    upd = lax.dot_general(kw4.astype(BF16), v4b, (((0,), (0,)), ((), ())), preferred_element_type=F32)
    dec_col = jnp.concatenate([decay_c[hh * L:hh * L + M_HEAD_DIM, :] for hh in range(H)], axis=0)
    bd = (lax.broadcasted_iota(jnp.int32, (M_W, M_W), 0) // M_HEAD_DIM
          == lax.broadcasted_iota(jnp.int32, (M_W, M_W), 1) // M_HEAD_DIM)
    c_st[...] = jnp.where(bd, dec_col * c_prev + upd, 0.0)
    lane_h = lax.broadcasted_iota(jnp.int32, (1, M_W), 1) // M_HEAD_DIM
    dec_row = jnp.zeros((1, M_W), F32)
    for hh in range(H):
        dec_row = jnp.where(lane_h == hh, jnp.broadcast_to(decay_c[hh * L:hh * L + 1, :], (1, M_W)), dec_row)
    n_st[...] = dec_row * n_st[...] + jnp.sum(kw4, axis=0, keepdims=True)
    m_st[...] = m_new_c


def _mix(rest, gatest, lw, B, S, L):
    T = rest.shape[0]
    nc = S // L
    full = lambda a: pl.BlockSpec(a.shape, lambda b, c: (0,) * a.ndim)
    weights = [lw["wpool"], lw["spool"], lw["wconv"], lw["bconv"], lw["wmq"], lw["wmk"], lw["gbias"], lw["gmh"]]
    return pl.pallas_call(
        functools.partial(_mix_kernel, L=L),
        grid=(B, nc),
        in_specs=[pl.BlockSpec((L, 4 * 256), lambda b, c: (b * nc + c, 0)),
                  pl.BlockSpec((16, L), lambda b, c: (0, b * nc + c))] + [full(w) for w in weights],
        out_specs=pl.BlockSpec((L, POOL_W + M_W), lambda b, c: (b * nc + c, 0)),
        out_shape=jax.ShapeDtypeStruct((T, POOL_W + M_W), BF16),
        scratch_shapes=[pltpu.VMEM((POOL_HALO + L, POOL_W), F32),
                        pltpu.VMEM((CONV_HALO + L, M_W), F32),
                        pltpu.VMEM((M_W, M_W), F32),
                        pltpu.VMEM((1, M_W), F32),
                        pltpu.VMEM((M_HEADS * L, 1), F32)],
        compiler_params=_cparams(("arbitrary", "arbitrary")),
        name="pool_mlstm",
    )(rest, gatest, *weights)


def _attn_kernel(q_ref, k_ref, v_ref, o_ref, m_sc, acc_sc, *, t, nsub):
    qi = pl.program_id(2)
    for r in range(nsub):
        m_sc[r] = jnp.full((t, LANES), -jnp.inf, F32)
        acc_sc[r] = jnp.zeros((t, 2 * MLA_V), F32)

    def chain_step(r, j, masked):
        start = pl.multiple_of(j * t, t)
        q = q_ref[r * t:(r + 1) * t, :]
        k = k_ref[pl.ds(start, t), :]
        v = v_ref[pl.ds(start, t), :]
        s = lax.dot_general(q, k, (((1,), (1,)), ((), ())), preferred_element_type=F32)
        if masked:
            ri = lax.broadcasted_iota(jnp.int32, (t, t), 0)
            ci = lax.broadcasted_iota(jnp.int32, (t, t), 1)
            s = jnp.where(ci <= ri, s, -jnp.inf)
        m_old = m_sc[r]
        m_new = jnp.maximum(m_old, jnp.max(s, axis=-1, keepdims=True))
        p = jnp.exp2(s - pltpu.repeat(m_new, t // LANES, axis=1))
        alpha = jnp.exp2(m_old - m_new)
        pv = jnp.dot(p.astype(BF16), v, preferred_element_type=F32)
        acc_sc[r] = pltpu.repeat(alpha, 2, axis=1) * acc_sc[r] + pv
        m_sc[r] = m_new

    def body(i, carry):
        for jj in range(nsub):
            for r in range(nsub):
                chain_step(r, nsub * i + jj, False)
        return carry

    lax.fori_loop(0, qi, body, 0)
    for kt in range(nsub):
        for r in range(kt, nsub):
            chain_step(r, nsub * qi + kt, r == kt)
    for r in range(nsub):
        acc = acc_sc[r]
        o_ref[r * t:(r + 1) * t, :] = (acc[:, :MLA_V] / acc[:, MLA_V:]).astype(o_ref.dtype)


def _attention(qc, kc, vv, B, S, t, nsub):
    T = qc.shape[0]
    tq = t * nsub
    nq = S // tq
    return pl.pallas_call(
        functools.partial(_attn_kernel, t=t, nsub=nsub),
        grid=(B, MLA_HEADS, nq),
        in_specs=[pl.BlockSpec((tq, QK_W), lambda b, h, i: (b * nq + i, h)),
                  pl.BlockSpec((S, QK_W), lambda b, h, i: (b, h)),
                  pl.BlockSpec((S, 2 * MLA_V), lambda b, h, i: (b, h))],
        out_specs=pl.BlockSpec((tq, MLA_V), lambda b, h, i: (b * nq + i, h)),
        out_shape=jax.ShapeDtypeStruct((T, MLA_HEADS * MLA_V), BF16),
        scratch_shapes=[pltpu.VMEM((nsub, t, LANES), F32), pltpu.VMEM((nsub, t, 2 * MLA_V), F32)],
        compiler_params=_cparams(("arbitrary", "arbitrary", "arbitrary")),
        name="mla_attention",
    )(qc, kc, vv)


def _block_kernel(h_ref, a_ref, pm_ref, woa_ref, wopm_ref, g1_ref, b1_ref,
                  wa_ref, wg_ref, wfc_ref, bfc_ref, wd_ref, g2_ref, b2_ref, o_ref,
                  tail_sc, u_buf, *, ts, nsub, tf, nsb):
    i = pl.program_id(0)
    nchunk = D_FF // tf

    @pl.when((i % nsb) == 0)
    def _():
        tail_sc[...] = jnp.zeros((CONV_HALO, D_FF), F32)

    row8 = lax.broadcasted_iota(jnp.int32, (CONV_HALO, tf), 0)
    h1s, hbs = [], []
    for r in range(nsub):
        rows = slice(r * ts, (r + 1) * ts)
        mix = (jnp.dot(a_ref[rows, :], woa_ref[...], preferred_element_type=F32)
               + jnp.dot(pm_ref[rows, :], wopm_ref[...], preferred_element_type=F32))
        h1 = _layer_norm_rows(ALPHA * h_ref[rows, :] + mix, g1_ref[...], b1_ref[...])
        h1s.append(h1)
        hbs.append(h1.astype(BF16))

    def up(r, c):
        cols = slice(c * tf, (c + 1) * tf)
        g = jnp.dot(hbs[r], wg_ref[:, cols], preferred_element_type=F32)
        a = jnp.dot(hbs[r], wa_ref[:, cols], preferred_element_type=F32)
        return g, a

    def shifted(g, prev8, k):
        body = pltpu.roll(g, k, axis=0)
        top = jnp.where(row8 < k, pltpu.roll(prev8, k, axis=0), body[0:CONV_HALO, :])
        return jnp.concatenate([top, body[CONV_HALO:, :]], axis=0)

    order = [(r, c) for c in range(nchunk) for r in range(nsub)]
    nxt = up(*order[0])
    for n, (r, c) in enumerate(order):
        cols = slice(c * tf, (c + 1) * tf)
        g, a = nxt
        if n + 1 < len(order):
            nxt = up(*order[n + 1])
        prev8 = tail_sc[:, cols]
        tail_sc[:, cols] = g[ts - CONV_HALO:, :]
        conv = g * wfc_ref[FFN_CONV - 1:FFN_CONV, cols] + bfc_ref[:, cols]
        for k in range(1, FFN_CONV):
            conv = conv + shifted(g, prev8, k) * wfc_ref[FFN_CONV - 1 - k:FFN_CONV - k, cols]
        u_buf[r, :, cols] = (conv * jax.nn.sigmoid(conv) * a).astype(BF16)
    for r in range(nsub):
        f = jnp.dot(u_buf[r], wd_ref[...], preferred_element_type=F32)
        o_ref[r * ts:(r + 1) * ts, :] = _layer_norm_rows(ALPHA * h1s[r] + f, g2_ref[...], b2_ref[...])


def _block(h, a, pm, lw, S, ts, nsub, tf):
    T = h.shape[0]
    tm = ts * nsub
    nsb = S // tm
    const = lambda x: pl.BlockSpec(x.shape, lambda i: (0,) * x.ndim, pipeline_mode=pl.Buffered(1))
    weights = [lw["woa"], lw["wopm"], lw["ln1g"], lw["ln1b"], lw["wup_a"], lw["wup_g"], lw["wfc"], lw["bfc"],
               lw["wd"], lw["ln2g"], lw["ln2b"]]
    return pl.pallas_call(
        functools.partial(_block_kernel, ts=ts, nsub=nsub, tf=tf, nsb=nsb),
        grid=(T // tm,),
        in_specs=[pl.BlockSpec((tm, D_MODEL), lambda i: (i, 0)),
                  pl.BlockSpec((tm, a.shape[1]), lambda i: (i, 0)),
                  pl.BlockSpec((tm, pm.shape[1]), lambda i: (i, 0))] + [const(w) for w in weights],
        out_specs=pl.BlockSpec((tm, D_MODEL), lambda i: (i, 0)),
        out_shape=jax.ShapeDtypeStruct((T, D_MODEL), F32),
        scratch_shapes=[pltpu.VMEM((CONV_HALO, D_FF), F32),
                        pltpu.VMEM((nsub, ts, D_FF), BF16)],
        compiler_params=_cparams(("arbitrary",)),
        name="outproj_ffn",
    )(h, a, pm, *weights)


def _block_diag(w):
    G, C, _ = w.shape
    out = jnp.zeros((G * C, G * C), w.dtype)
    for g in range(G):
        out = out.at[g * C:(g + 1) * C, g * C:(g + 1) * C].set(w[g])
    return out


def _prep_layer(l, w_in, g_qn, w_uq, g_kvn, w_ukv, w_pool, s_pool, w_mconv, b_mconv, w_mq, w_mk,
                b_i, b_f, g_mh, w_out, ln1_g, ln1_b, w_up, w_fconv, b_fconv, w_down, ln2_g, ln2_b):
    half = MLA_ROPE // 2
    wi = w_in[l]
    o_kr = Q_LORA + KV_LORA
    o_rest = o_kr + MLA_ROPE
    o_gate = o_rest + 4 * 256
    kr = wi[:, o_kr:o_kr + MLA_ROPE]
    zpad = jnp.zeros((D_MODEL, LANES - MLA_ROPE), F32)
    kr_sw = jnp.concatenate([-kr[:, half:], kr[:, :half]], axis=1)
    wa = jnp.concatenate([wi[:, :o_kr], kr, zpad, kr_sw, zpad], axis=1)
    wr = wi[:, o_rest:o_gate]
    wgate = wi[:, o_gate:o_gate + 2 * M_HEADS]
    wgt = jnp.concatenate([wgate.T, jnp.zeros((16 - 2 * M_HEADS, D_MODEL), F32)], axis=0)

    wuq = w_uq[l].reshape(Q_LORA, MLA_HEADS, MLA_NOPE + MLA_ROPE)
    zq = jnp.zeros((Q_LORA, MLA_HEADS, LANES - MLA_ROPE), F32)
    w1 = jnp.concatenate([wuq, zq], axis=2).reshape(Q_LORA, MLA_HEADS * QK_W)
    rope = wuq[:, :, MLA_NOPE:]
    w2 = jnp.concatenate([-rope[:, :, half:], rope[:, :, :half], zq], axis=2).reshape(Q_LORA, MLA_HEADS * LANES)
    wq = jnp.concatenate([w1, w2], axis=1)

    wukv = w_ukv[l].reshape(KV_LORA, MLA_HEADS, MLA_NOPE + MLA_V)
    wkv = jnp.concatenate([wukv[:, :, :MLA_NOPE].reshape(KV_LORA, -1),
                           wukv[:, :, MLA_NOPE:].reshape(KV_LORA, -1)], axis=1)

    gbias = jnp.concatenate([b_i[l], b_f[l], jnp.zeros((16 - 2 * M_HEADS,), F32)]).reshape(16, 1)
    wo = w_out[l]
    return dict(
        wa=wa.astype(BF16), wr=wr.astype(BF16), wgt=wgt.astype(BF16),
        gq=g_qn[l].reshape(1, Q_LORA), gkv=g_kvn[l].reshape(1, KV_LORA),
        wq=wq.astype(BF16), wkv=wkv.astype(BF16),
        wpool=_block_diag(w_pool[l]).astype(BF16), spool=s_pool[l].reshape(1, POOL_W),
        wconv=w_mconv[l], bconv=b_mconv[l].reshape(1, M_W),
        wmq=_block_diag(w_mq[l]).astype(BF16), wmk=_block_diag(w_mk[l]).astype(BF16),
        gbias=gbias, gmh=g_mh[l].reshape(1, M_W),
        woa=wo[:MLA_HEADS * MLA_V].astype(BF16), wopm=wo[MLA_HEADS * MLA_V:].astype(BF16),
        ln1g=ln1_g[l].reshape(1, D_MODEL), ln1b=ln1_b[l].reshape(1, D_MODEL),
        wup_a=w_up[l][:, :D_FF].astype(BF16), wup_g=w_up[l][:, D_FF:].astype(BF16),
        wfc=w_fconv[l], bfc=b_fconv[l].reshape(1, D_FF), wd=w_down[l].astype(BF16),
        ln2g=ln2_g[l].reshape(1, D_MODEL), ln2b=ln2_b[l].reshape(1, D_MODEL),
    )


def _rope_tables(S):
    inv = 1.0 / (ROPE_BASE ** (jnp.arange(0, MLA_ROPE, 2, dtype=F32) / MLA_ROPE))
    ang = jnp.arange(S, dtype=F32)[:, None] * inv[None, :]
    z = jnp.zeros((S, LANES - MLA_ROPE), F32)
    ctab = jnp.concatenate([jnp.cos(ang), jnp.cos(ang), z], axis=1)
    stab = jnp.concatenate([jnp.sin(ang), jnp.sin(ang), z], axis=1)
    return ctab, stab


def _tile(pref, n):
    t = min(pref, n)
    assert n % t == 0, (pref, n)
    return t


def kernel(x, ln0_g, ln0_b, w_in, g_qn, w_uq, g_kvn, w_ukv, w_pool, s_pool, w_mconv, b_mconv, w_mq, w_mk, b_i, b_f, g_mh, w_out, ln1_g, ln1_b, w_up, w_fconv, b_fconv, w_down, ln2_g, ln2_b):
    B, S, D = x.shape
    assert D == D_MODEL
    T = B * S
    params = (w_in, g_qn, w_uq, g_kvn, w_ukv, w_pool, s_pool, w_mconv, b_mconv, w_mq, w_mk, b_i, b_f,
              g_mh, w_out, ln1_g, ln1_b, w_up, w_fconv, b_fconv, w_down, ln2_g, ln2_b)
    ctab, stab = _rope_tables(S)
    h = _input_ln(x.reshape(T, D), ln0_g, ln0_b, _tile(PROJ_TM, S))
    for l in range(DEPTH):
        lw = _prep_layer(l, *params)
        qc, kc, vv, rest, gatest = _proj(h, ctab, stab, lw, S, _tile(PROJ_TM, S))
        pm = _mix(rest, gatest, lw, B, S, _tile(MIX_L, S))
        a = _attention(qc, kc, vv, B, S, _tile(ATT_T, S // ATT_NSUB), ATT_NSUB)
        h = _block(h, a, pm, lw, S, _tile(FFN_TM, S // FFN_NSUB), FFN_NSUB, FFN_TF)
    return h.reshape(B, S, D)
```

```python
import functools

import jax
import jax.numpy as jnp
import numpy as np
from jax import lax
from jax.experimental import pallas as pl
from jax.experimental.pallas import tpu as pltpu

D_MODEL = 1024
DEPTH = 4
MLA_HEADS = 4
MLA_NOPE = 128
MLA_ROPE = 64
MLA_V = 128
Q_LORA = 384
KV_LORA = 256
ROPE_BASE = 10000.0
POOL_WINDOWS = (2, 4, 8, 16)
POOL_CH = 64
POOL_W = 256
M_HEADS = 4
M_HEAD_DIM = 64
M_W = 256
M_CONV = 4
D_FF = 2816
FFN_CONV = 3
LN_EPS = 1e-5
RMS_EPS = 1e-6
ALPHA = (2 * DEPTH) ** 0.25
ATT_SCALE = (MLA_NOPE + MLA_ROPE) ** -0.5
LOG2_E = 1.4426950408889634
Q_SCALE = ATT_SCALE * LOG2_E

LANES = 128
QK_W = 2 * LANES
Q_HEAD_COLS = 3 * LANES
POOL_HALO = 16
CONV_HALO = 8
REST_W = POOL_W + 2 * M_W
GATE_ROWS = 16
VMEM_LIMIT_BYTES = 56 * 1024 * 1024

PROJ_TM = 512
MIX_L = 256
ATT_T = 512
ATT_NSUB = 4
FFN_TM = 512
FFN_NSUB = 2
FFN_TF = 256

BF16 = jnp.bfloat16
F32 = jnp.float32


def _cparams(sem):
    return pltpu.CompilerParams(dimension_semantics=sem, vmem_limit_bytes=VMEM_LIMIT_BYTES)


def _layer_norm_rows(x, g, b):
    mu = jnp.mean(x, axis=-1, keepdims=True)
    d = x - mu
    var = jnp.mean(d * d, axis=-1, keepdims=True)
    return d * lax.rsqrt(var + LN_EPS) * g + b


def _ln_kernel(x_ref, g_ref, b_ref, o_ref):
    o_ref[...] = _layer_norm_rows(x_ref[...], g_ref[...], b_ref[...])


def _input_ln(x2, g, b, tm):
    T, D = x2.shape
    return pl.pallas_call(
        _ln_kernel,
        grid=(T // tm,),
        in_specs=[pl.BlockSpec((tm, D), lambda i: (i, 0)),
                  pl.BlockSpec((1, D), lambda i: (0, 0)),
                  pl.BlockSpec((1, D), lambda i: (0, 0))],
        out_specs=pl.BlockSpec((tm, D), lambda i: (i, 0)),
        out_shape=jax.ShapeDtypeStruct((T, D), F32),
        compiler_params=_cparams(("arbitrary",)),
        name="input_ln",
    )(x2, g.reshape(1, D), b.reshape(1, D))


def _proj_kernel(h_ref, ct_ref, st_ref, wa_ref, wr_ref, wvt_ref, wgt_ref, gq_ref, gkv_ref,
                 wq_ref, wkv_ref,
                 wpool_ref, spool_ref, wconv_ref, bconv_ref, wmq_ref, wmk_ref, gbias_ref, gbiasr_ref, gmh_ref,
                 q_ref, k_ref, v_ref, pm_ref,
                 xp_buf, xc_buf, c_st, n_st, m_st, *, nsb, L):
    i = pl.program_id(0)
    tm = h_ref.shape[0]
    hb = h_ref[...].astype(BF16)
    rest = jnp.dot(hb, wr_ref[...], preferred_element_type=F32)
    nt = (((1,), (1,)), ((), ()))
    vt = lax.dot_general(wvt_ref[...], hb, nt, preferred_element_type=F32).astype(BF16)
    gt = lax.dot_general(wgt_ref[...], hb, nt, preferred_element_type=F32)
    pq = jnp.dot(hb, wa_ref[:, :Q_LORA + LANES], preferred_element_type=F32)
    gc = pq[:, Q_LORA:]

    @pl.when((i % nsb) == 0)
    def _():
        xp_buf[0:POOL_HALO, :] = jnp.zeros((POOL_HALO, POOL_W), F32)
        xc_buf[0:CONV_HALO, :] = jnp.zeros((CONV_HALO, M_W), F32)
        c_st[...] = jnp.zeros_like(c_st)
        n_st[...] = jnp.zeros_like(n_st)
        m_st[...] = jnp.zeros_like(m_st)

    ct = ct_ref[...]
    st = st_ref[...]
    vals = {}

    def mla_q_latent():
        cq = pq[:, :Q_LORA]
        nq = cq * lax.rsqrt(jnp.mean(cq * cq, axis=-1, keepdims=True) + RMS_EPS) * gq_ref[...]
        vals["nq"] = nq.astype(BF16)

    def mla_q_heads(h0, h1):
        def emit():
            for hd in range(h0, h1):
                base = hd * Q_HEAD_COLS
                y = jnp.dot(vals["nq"], wq_ref[:, base:base + Q_HEAD_COLS], preferred_element_type=F32)
                q_ref[:, hd * QK_W:hd * QK_W + LANES] = (y[:, :LANES] * Q_SCALE).astype(BF16)
                rot = y[:, LANES:2 * LANES] * ct + y[:, 2 * LANES:] * st
                q_ref[:, hd * QK_W + LANES:(hd + 1) * QK_W] = (rot * Q_SCALE).astype(BF16)
        return emit

    def mla_kv_latent():
        pkv = jnp.dot(hb, wa_ref[:, Q_LORA + LANES:], preferred_element_type=F32)
        ckv = pkv[:, :KV_LORA]
        nkv = ckv * lax.rsqrt(jnp.mean(ckv * ckv, axis=-1, keepdims=True) + RMS_EPS) * gkv_ref[...]
        vals["nkv"] = nkv.astype(BF16)
        vals["kpe"] = (pkv[:, KV_LORA:KV_LORA + LANES] * ct + pkv[:, KV_LORA + LANES:] * st).astype(BF16)

    def mla_k():
        yk = jnp.dot(vals["nkv"], wkv_ref[:, :MLA_HEADS * MLA_NOPE], preferred_element_type=F32)
        for hd in range(MLA_HEADS):
            k_ref[:, hd * QK_W:hd * QK_W + LANES] = yk[:, hd * LANES:(hd + 1) * LANES].astype(BF16)
            k_ref[:, hd * QK_W + LANES:(hd + 1) * QK_W] = vals["kpe"]

    def mla_v():
        yv = jnp.dot(vals["nkv"], wkv_ref[:, MLA_HEADS * MLA_NOPE:], preferred_element_type=F32)
        ones = jnp.ones((tm, MLA_V), BF16)
        for hd in range(MLA_HEADS):
            v_ref[:, hd * QK_W:hd * QK_W + MLA_V] = yv[:, hd * MLA_V:(hd + 1) * MLA_V].astype(BF16)
            v_ref[:, hd * QK_W + MLA_V:(hd + 1) * QK_W] = ones

    fillers = [mla_q_latent, mla_q_heads(0, 2), mla_q_heads(2, 4), mla_kv_latent, mla_k, mla_v]
    for ci in range(tm // L):
        rows = slice(ci * L, (ci + 1) * L)
        _mix_chunk(rest[rows, 0:POOL_W], rest[rows, POOL_W:POOL_W + M_W], rest[rows, POOL_W + M_W:REST_W],
                   vt[:, rows], gt[:, rows], gc[rows, :], (i % nsb) * tm + ci * L, rows, pm_ref,
                   wpool_ref, spool_ref, wconv_ref, bconv_ref, wmq_ref, wmk_ref, gbias_ref, gbiasr_ref, gmh_ref,
                   xp_buf, xc_buf, c_st, n_st, m_st, fillers)
    while fillers:
        fillers.pop(0)()


def _proj(h, ctab, stab, lw, S, tm, L):
    T = h.shape[0]
    nsb = S // tm
    full = lambda a: pl.BlockSpec(a.shape, lambda i: (0,) * a.ndim)
    weights = [lw["wa"], lw["wr"], lw["wvt"], lw["wgt"], lw["gq"], lw["gkv"], lw["wq"], lw["wkv"],
               lw["wpool"], lw["spool"], lw["wconv"], lw["bconv"], lw["wmq"], lw["wmk"], lw["gbias"],
               lw["gbias_row"], lw["gmh"]]
    HW = MLA_HEADS * QK_W
    return pl.pallas_call(
        functools.partial(_proj_kernel, nsb=nsb, L=L),
        grid=(T // tm,),
        in_specs=[pl.BlockSpec((tm, D_MODEL), lambda i: (i, 0)),
                  pl.BlockSpec((tm, LANES), lambda i: (i % nsb, 0)),
                  pl.BlockSpec((tm, LANES), lambda i: (i % nsb, 0))] + [full(w) for w in weights],
        out_specs=[pl.BlockSpec((tm, HW), lambda i: (i, 0)),
                   pl.BlockSpec((tm, HW), lambda i: (i, 0)),
                   pl.BlockSpec((tm, HW), lambda i: (i, 0)),
                   pl.BlockSpec((tm, POOL_W + M_W), lambda i: (i, 0))],
        out_shape=[jax.ShapeDtypeStruct((T, HW), BF16),
                   jax.ShapeDtypeStruct((T, HW), BF16),
                   jax.ShapeDtypeStruct((T, HW), BF16),
                   jax.ShapeDtypeStruct((T, POOL_W + M_W), BF16)],
        scratch_shapes=[pltpu.VMEM((POOL_HALO + L, POOL_W), F32),
                        pltpu.VMEM((CONV_HALO + L, M_W), F32),
                        pltpu.VMEM((M_W, M_W), F32),
                        pltpu.VMEM((1, M_W), F32),
                        pltpu.VMEM((2 * M_HEADS, LANES), F32)],
        compiler_params=_cparams(("arbitrary",)),
        name="proj_mix",
    )(h, ctab, stab, *weights)


def _split3(x):
    hi = x.astype(BF16)
    r = x - hi.astype(F32)
    mid = r.astype(BF16)
    lo = (r - mid.astype(F32)).astype(BF16)
    return hi, mid, lo


def _mix_chunk(xp, xqk, xo, vt, gt_raw, gc_raw, tpos0, rows, o_ref,
               wpool_ref, spool_ref, wconv_ref, bconv_ref, wmq_ref, wmk_ref, gbias_ref, gbiasr_ref, gmh_ref,
               xp_buf, xc_buf, c_st, n_st, m_st, fillers):
    L = xp.shape[0]
    H = M_HEADS

    xp_buf[POOL_HALO:POOL_HALO + L, :] = xp
    lane = lax.broadcasted_iota(jnp.int32, (L, POOL_W), 1)
    grp = lane // POOL_CH
    acc = xp_buf[...]
    wsum = jnp.zeros((L, POOL_W), F32)
    for gi, w in enumerate(POOL_WINDOWS):
        assert w == 2 ** (gi + 1)
        acc = acc + pltpu.roll(acc, w // 2, axis=0)
        wsum = jnp.where(grp == gi, acc[POOL_HALO:, :], wsum)
    win = jnp.where(grp == 0, POOL_WINDOWS[0],
                    jnp.where(grp == 1, POOL_WINDOWS[1],
                              jnp.where(grp == 2, POOL_WINDOWS[2], POOL_WINDOWS[3])))
    tpos = tpos0 + lax.broadcasted_iota(jnp.int32, (L, POOL_W), 0)
    cnt = jnp.minimum(tpos + 1, win).astype(F32)
    y = wsum / cnt - xp
    pooled = jnp.dot(y.astype(BF16), wpool_ref[...], preferred_element_type=F32) * spool_ref[...]
    o_ref[rows, 0:POOL_W] = pooled.astype(BF16)
    xp_buf[0:POOL_HALO, :] = xp[L - POOL_HALO:, :]

    def next_filler():
        if fillers:
            fillers.pop(0)()

    next_filler()
    xc_buf[CONV_HALO:CONV_HALO + L, :] = xqk
    conv = xqk * wconv_ref[M_CONV - 1:M_CONV, :] + bconv_ref[...]
    for k in range(1, M_CONV):
        conv = conv + xc_buf[CONV_HALO - k:CONV_HALO - k + L, :] * wconv_ref[M_CONV - 1 - k:M_CONV - k, :]
    xc_buf[0:CONV_HALO, :] = xqk[L - CONV_HALO:, :]
    cs = (conv * jax.nn.sigmoid(conv)).astype(BF16)
    q = jnp.dot(cs, wmq_ref[...], preferred_element_type=F32)
    kk = jnp.dot(cs, wmk_ref[...], preferred_element_type=F32) * (M_HEAD_DIM ** -0.5)

    tri_r = lax.broadcasted_iota(jnp.int32, (L, L), 0)
    tri_c = lax.broadcasted_iota(jnp.int32, (L, L), 1)
    causal = tri_r <= tri_c

    def cumsum3(lhs_parts, rhs_parts):
        out = None
        for lp, rp in zip(lhs_parts, rhs_parts):
            d = jnp.dot(lp, rp, preferred_element_type=F32)
            out = d if out is None else out + d
        return out

    upper = causal.astype(BF16)
    lower = (tri_c <= tri_r).astype(BF16)
    gt = gt_raw + gbias_ref[...]
    li_r = gt[0:H, :]
    lf_r = jax.nn.log_sigmoid(gt)
    b_r = cumsum3(_split3(lf_r), [upper] * 3)[H:2 * H, :]
    gc = gc_raw + gbiasr_ref[...]
    li_c = gc
    lf_c = jax.nn.log_sigmoid(pltpu.roll(gc, LANES // 2, axis=1))
    b_c = cumsum3([lower] * 3, _split3(lf_c))
    a_c = li_c - b_c

    lane_r = lax.broadcasted_iota(jnp.int32, (1, LANES), 1)
    m_prev = m_st[...]
    e_blocks, g_rows, eneg_rows, mt_last, b_last = [], [], [], [], []
    for hh in range(H):
        a_col = jnp.broadcast_to(a_c[:, hh:hh + 1], (L, L))
        cm = jnp.max(jnp.where(causal, a_col, -jnp.inf), axis=0, keepdims=True)
        mp = jnp.concatenate([m_prev[hh:hh + 1, :]] * (L // LANES), axis=1)
        m_row = jnp.maximum(mp, cm)
        e_blocks.append(jnp.exp(jnp.where(causal, a_col - m_row, -jnp.inf)))
        g_rows.append(jnp.exp(mp - m_row))
        mt = b_r[hh:hh + 1, :] + m_row
        eneg_rows.append(jnp.exp(-mt))
        mt_last.append(mt[:, L - 1:L])
        b_last.append(b_r[hh:hh + 1, L - 1:L])
    ET = jnp.concatenate(e_blocks, axis=0)
    next_filler()

    hlane = lax.broadcasted_iota(jnp.int32, (L, M_W), 1) // M_HEAD_DIM
    kb = kk.astype(BF16)
    qb = q.astype(BF16)
    zero_b = jnp.zeros((L, M_W), BF16)
    k4 = jnp.concatenate([jnp.where(hlane == hh, kb, zero_b) for hh in range(H)], axis=0)
    nt = (((1,), (1,)), ((), ()))
    scT = lax.dot_general(k4, qb, nt, preferred_element_type=F32) * ET
    den_rows = [jnp.sum(scT[hh * L:(hh + 1) * L, :], axis=0, keepdims=True) for hh in range(H)]
    hrow =lax.broadcasted_iota(jnp.int32, (M_W, L), 0) // M_HEAD_DIM
    zero_v = jnp.zeros((M_W, L), BF16)
    vt4 = jnp.concatenate([jnp.where(hrow == hh, vt, zero_v) for hh in range(H)], axis=1)
    numT = jnp.dot(vt4, scT.astype(BF16), preferred_element_type=F32)
    ct_prev = c_st[...]
    interT = lax.dot_general(ct_prev.astype(BF16), qb, nt, preferred_element_type=F32)
    n_prev = n_st[...]
    nrow = lax.broadcasted_iota(jnp.int32, (GATE_ROWS, M_W), 0)
    nlane = lax.broadcasted_iota(jnp.int32, (GATE_ROWS, M_W), 1) // M_HEAD_DIM
    n4 = jnp.where(nrow == nlane, jnp.broadcast_to(n_prev, (GATE_ROWS, M_W)), 0.0).astype(BF16)
    qn = lax.dot_general(n4, qb, nt, preferred_element_type=F32)

    y_blocks = []
    for hh in range(H):
        erows = slice(hh * M_HEAD_DIM, (hh + 1) * M_HEAD_DIM)
        den = g_rows[hh] * qn[hh:hh + 1, :] + den_rows[hh]
        rinv = 1.0 / jnp.maximum(jnp.abs(den), eneg_rows[hh])
        hblk = (g_rows[hh] * interT[erows, :] + numT[erows, :]) * rinv
        mu = jnp.mean(hblk, axis=0, keepdims=True)
        dd = hblk - mu
        var = jnp.mean(dd * dd, axis=0, keepdims=True)
        y_blocks.append(dd * lax.rsqrt(var + LN_EPS))
    yn = jnp.concatenate(y_blocks, axis=0).T
    o_ref[rows, POOL_W:POOL_W + M_W] = (jax.nn.sigmoid(xo) * (yn * gmh_ref[...])).astype(BF16)
    next_filler()

    lane_h = lax.broadcasted_iota(jnp.int32, (1, M_W), 1) // M_HEAD_DIM
    hv = jnp.zeros((1, LANES), F32)
    dec_row = jnp.zeros((1, M_W), F32)
    for hh in range(H):
        m_new = mt_last[hh]
        hv = jnp.where(lane_r == hh, b_last[hh] - m_new, hv)
        decay = jnp.exp(b_last[hh] + m_prev[hh:hh + 1, 0:1] - m_new)
        dec_row = jnp.where(lane_h == hh, decay, dec_row)
        m_st[hh:hh + 1, :] = jnp.broadcast_to(m_new, (1, LANES))
    w_c = jnp.exp(a_c + hv)
    wexp = jnp.zeros((L, M_W), F32)
    for hh in range(H):
        wexp = jnp.where(hlane == hh, jnp.broadcast_to(w_c[:, hh:hh + 1], (L, M_W)), wexp)
    kw = kk * wexp
    upd = jnp.dot(vt, kw.astype(BF16), preferred_element_type=F32)
    bd = (lax.broadcasted_iota(jnp.int32, (M_W, M_W), 0) // M_HEAD_DIM
          == lax.broadcasted_iota(jnp.int32, (M_W, M_W), 1) // M_HEAD_DIM)
    c_st[...] = jnp.where(bd, dec_row * ct_prev + upd, 0.0)
    n_st[...] = dec_row * n_prev + jnp.sum(kw, axis=0, keepdims=True)


def _attn_kernel(q_ref, k_ref, v_ref, o_ref, m_sc, acc_sc, *, t, nsub):
    qi = pl.program_id(2)
    for r in range(nsub):
        m_sc[r] = jnp.full((t, LANES), -jnp.inf, F32)
        acc_sc[r] = jnp.zeros((t, 2 * MLA_V), F32)

    def chain_step(r, j, masked):
        start = pl.multiple_of(j * t, t)
        q = q_ref[r * t:(r + 1) * t, :]
        k = k_ref[pl.ds(start, t), :]
        v = v_ref[pl.ds(start, t), :]
        s = lax.dot_general(q, k, (((1,), (1,)), ((), ())), preferred_element_type=F32)
        if masked:
            ri = lax.broadcasted_iota(jnp.int32, (t, t), 0)
            ci = lax.broadcasted_iota(jnp.int32, (t, t), 1)
            s = jnp.where(ci <= ri, s, -jnp.inf)
        m_old = m_sc[r]
        m_new = jnp.maximum(m_old, jnp.max(s, axis=-1, keepdims=True))
        p = jnp.exp2(s - jnp.concatenate([m_new] * (t // LANES), axis=1))
        alpha = jnp.exp2(m_old - m_new)
        pv = jnp.dot(p.astype(BF16), v, preferred_element_type=F32)
        acc_sc[r] = jnp.concatenate([alpha, alpha], axis=1) * acc_sc[r] + pv
        m_sc[r] = m_new

    def body(i, carry):
        for jj in range(nsub):
            for r in range(nsub):
                chain_step(r, nsub * i + jj, False)
        return carry

    lax.fori_loop(0, qi, body, 0)
    for kt in range(nsub):
        for r in range(kt, nsub):
            chain_step(r, nsub * qi + kt, r == kt)
    for r in range(nsub):
        acc = acc_sc[r]
        o_ref[r * t:(r + 1) * t, :] = (acc[:, :MLA_V] / acc[:, MLA_V:]).astype(o_ref.dtype)


def _attention(qc, kc, vv, B, S, t, nsub):
    T = qc.shape[0]
    tq = t * nsub
    nq = S // tq
    return pl.pallas_call(
        functools.partial(_attn_kernel, t=t, nsub=nsub),
        grid=(B, MLA_HEADS, nq),
        in_specs=[pl.BlockSpec((tq, QK_W), lambda b, h, i: (b * nq + i, h)),
                  pl.BlockSpec((S, QK_W), lambda b, h, i: (b, h)),
                  pl.BlockSpec((S, 2 * MLA_V), lambda b, h, i: (b, h))],
        out_specs=pl.BlockSpec((tq, MLA_V), lambda b, h, i: (b * nq + i, h)),
        out_shape=jax.ShapeDtypeStruct((T, MLA_HEADS * MLA_V), BF16),
        scratch_shapes=[pltpu.VMEM((nsub, t, LANES), F32), pltpu.VMEM((nsub, t, 2 * MLA_V), F32)],
        compiler_params=_cparams(("arbitrary", "arbitrary", "arbitrary")),
        name="mla_attention",
    )(qc, kc, vv)


def _block_kernel(h_ref, a_ref, pm_ref, woa_ref, wopm_ref, g1_ref, b1_ref,
                  wa_ref, wg_ref, wfc_ref, bfc_ref, wd_ref, g2_ref, b2_ref, o_ref,
                  tail_sc, u_buf, *, ts, nsub, tf, nsb):
    i = pl.program_id(0)
    nchunk = D_FF // tf

    @pl.when((i % nsb) == 0)
    def _():
        tail_sc[...] = jnp.zeros((CONV_HALO, D_FF), F32)

    row8 = lax.broadcasted_iota(jnp.int32, (CONV_HALO, tf), 0)
    h1s, hbs = [], []
    for r in range(nsub):
        rows = slice(r * ts, (r + 1) * ts)
        mix = (jnp.dot(a_ref[rows, :], woa_ref[...], preferred_element_type=F32)
               + jnp.dot(pm_ref[rows, :], wopm_ref[...], preferred_element_type=F32))
        h1 = _layer_norm_rows(ALPHA * h_ref[rows, :] + mix, g1_ref[...], b1_ref[...])
        h1s.append(h1)
        hbs.append(h1.astype(BF16))

    def up(r, c):
        cols = slice(c * tf, (c + 1) * tf)
        g = jnp.dot(hbs[r], wg_ref[:, cols], preferred_element_type=F32)
        a = jnp.dot(hbs[r], wa_ref[:, cols], preferred_element_type=F32)
        return g, a

    def shifted(g, prev8, k):
        body = pltpu.roll(g, k, axis=0)
        top = jnp.where(row8 < k, pltpu.roll(prev8, k, axis=0), body[0:CONV_HALO, :])
        return jnp.concatenate([top, body[CONV_HALO:, :]], axis=0)

    order = [(r, c) for c in range(nchunk) for r in range(nsub)]
    nxt = up(*order[0])
    for n, (r, c) in enumerate(order):
        cols = slice(c * tf, (c + 1) * tf)
        g, a = nxt
        if n + 1 < len(order):
            nxt = up(*order[n + 1])
        prev8 = tail_sc[:, cols]
        tail_sc[:, cols] = g[ts - CONV_HALO:, :]
        conv = g * wfc_ref[FFN_CONV - 1:FFN_CONV, cols] + bfc_ref[:, cols]
        for k in range(1, FFN_CONV):
            conv = conv + shifted(g, prev8, k) * wfc_ref[FFN_CONV - 1 - k:FFN_CONV - k, cols]
        u_buf[r, :, cols] = (conv * jax.nn.sigmoid(conv) * a).astype(BF16)
    for r in range(nsub):
        f = jnp.dot(u_buf[r], wd_ref[...], preferred_element_type=F32)
        o_ref[r * ts:(r + 1) * ts, :] = _layer_norm_rows(ALPHA * h1s[r] + f, g2_ref[...], b2_ref[...])


def _block(h, a, pm, lw, S, ts, nsub, tf):
    T = h.shape[0]
    tm = ts * nsub
    nsb = S // tm
    const = lambda x: pl.BlockSpec(x.shape, lambda i: (0,) * x.ndim, pipeline_mode=pl.Buffered(1))
    weights = [lw["woa"], lw["wopm"], lw["ln1g"], lw["ln1b"], lw["wup_a"], lw["wup_g"], lw["wfc"], lw["bfc"],
               lw["wd"], lw["ln2g"], lw["ln2b"]]
    return pl.pallas_call(
        functools.partial(_block_kernel, ts=ts, nsub=nsub, tf=tf, nsb=nsb),
        grid=(T // tm,),
        in_specs=[pl.BlockSpec((tm, D_MODEL), lambda i: (i, 0)),
                  pl.BlockSpec((tm, a.shape[1]), lambda i: (i, 0)),
                  pl.BlockSpec((tm, pm.shape[1]), lambda i: (i, 0))] + [const(w) for w in weights],
        out_specs=pl.BlockSpec((tm, D_MODEL), lambda i: (i, 0)),
        out_shape=jax.ShapeDtypeStruct((T, D_MODEL), F32),
        scratch_shapes=[pltpu.VMEM((CONV_HALO, D_FF), F32),
                        pltpu.VMEM((nsub, ts, D_FF), BF16)],
        compiler_params=_cparams(("arbitrary",)),
        name="outproj_ffn",
    )(h, a, pm, *weights)


def _block_diag(w):
    G, C, _ = w.shape
    out = jnp.zeros((G * C, G * C), w.dtype)
    for g in range(G):
        out = out.at[g * C:(g + 1) * C, g * C:(g + 1) * C].set(w[g])
    return out


def _prep_layer(l, w_in, g_qn, w_uq, g_kvn, w_ukv, w_pool, s_pool, w_mconv, b_mconv, w_mq, w_mk,
                b_i, b_f, g_mh, w_out, ln1_g, ln1_b, w_up, w_fconv, b_fconv, w_down, ln2_g, ln2_b):
    half = MLA_ROPE // 2
    wi = w_in[l]
    o_kr = Q_LORA + KV_LORA
    o_rest = o_kr + MLA_ROPE
    o_gate = o_rest + 4 * 256
    kr = wi[:, o_kr:o_kr + MLA_ROPE]
    zpad = jnp.zeros((D_MODEL, LANES - MLA_ROPE), F32)
    kr_sw = jnp.concatenate([-kr[:, half:], kr[:, :half]], axis=1)
    wgate = wi[:, o_gate:o_gate + 2 * M_HEADS]
    zg = jnp.zeros((D_MODEL, LANES // 2 - M_HEADS), F32)
    wgc = jnp.concatenate([wgate[:, :M_HEADS], zg, wgate[:, M_HEADS:], zg], axis=1)
    wa = jnp.concatenate([wi[:, :Q_LORA], wgc, wi[:, Q_LORA:o_kr], kr, zpad, kr_sw, zpad], axis=1)
    o_xv = o_rest + POOL_W + M_W
    wr = jnp.concatenate([wi[:, o_rest:o_xv], wi[:, o_xv + M_W:o_gate]], axis=1)
    wvt = wi[:, o_xv:o_xv + M_W].T
    wgt = jnp.concatenate([wgate.T, jnp.zeros((GATE_ROWS - 2 * M_HEADS, D_MODEL), F32)], axis=0)
    zb = jnp.zeros((LANES // 2 - M_HEADS,), F32)
    gbias_row = jnp.concatenate([b_i[l], zb, b_f[l], zb]).reshape(1, LANES)

    wuq = w_uq[l].reshape(Q_LORA, MLA_HEADS, MLA_NOPE + MLA_ROPE)
    zq = jnp.zeros((Q_LORA, MLA_HEADS, LANES - MLA_ROPE), F32)
    rope = wuq[:, :, MLA_NOPE:]
    wq = jnp.concatenate([wuq, zq, -rope[:, :, half:], rope[:, :, :half], zq],
                         axis=2).reshape(Q_LORA, MLA_HEADS * Q_HEAD_COLS)

    wukv = w_ukv[l].reshape(KV_LORA, MLA_HEADS, MLA_NOPE + MLA_V)
    wkv = jnp.concatenate([wukv[:, :, :MLA_NOPE].reshape(KV_LORA, -1),
                           wukv[:, :, MLA_NOPE:].reshape(KV_LORA, -1)], axis=1)

    gbias = jnp.concatenate([b_i[l], b_f[l], jnp.zeros((GATE_ROWS - 2 * M_HEADS,), F32)]).reshape(GATE_ROWS, 1)
    wo = w_out[l]
    return dict(
        wa=wa.astype(BF16), wr=wr.astype(BF16), wvt=wvt.astype(BF16), wgt=wgt.astype(BF16),
        gbias_row=gbias_row,
        gq=g_qn[l].reshape(1, Q_LORA), gkv=g_kvn[l].reshape(1, KV_LORA),
        wq=wq.astype(BF16), wkv=wkv.astype(BF16),
        wpool=_block_diag(w_pool[l]).astype(BF16), spool=s_pool[l].reshape(1, POOL_W),
        wconv=w_mconv[l], bconv=b_mconv[l].reshape(1, M_W),
        wmq=_block_diag(w_mq[l]).astype(BF16), wmk=_block_diag(w_mk[l]).astype(BF16),
        gbias=gbias, gmh=g_mh[l].reshape(1, M_W),
        woa=wo[:MLA_HEADS * MLA_V].astype(BF16), wopm=wo[MLA_HEADS * MLA_V:].astype(BF16),
        ln1g=ln1_g[l].reshape(1, D_MODEL), ln1b=ln1_b[l].reshape(1, D_MODEL),
        wup_a=w_up[l][:, :D_FF].astype(BF16), wup_g=w_up[l][:, D_FF:].astype(BF16),
        wfc=w_fconv[l], bfc=b_fconv[l].reshape(1, D_FF), wd=w_down[l].astype(BF16),
        ln2g=ln2_g[l].reshape(1, D_MODEL), ln2b=ln2_b[l].reshape(1, D_MODEL),
    )


def _rope_tables(S):
    inv = 1.0 / (ROPE_BASE ** (jnp.arange(0, MLA_ROPE, 2, dtype=F32) / MLA_ROPE))
    ang = jnp.arange(S, dtype=F32)[:, None] * inv[None, :]
    z = jnp.zeros((S, LANES - MLA_ROPE), F32)
    ctab = jnp.concatenate([jnp.cos(ang), jnp.cos(ang), z], axis=1)
    stab = jnp.concatenate([jnp.sin(ang), jnp.sin(ang), z], axis=1)
    return ctab, stab


def _tile(pref, n):
    t = min(pref, n)
    assert n % t == 0, (pref, n)
    return t


def kernel(x, ln0_g, ln0_b, w_in, g_qn, w_uq, g_kvn, w_ukv, w_pool, s_pool, w_mconv, b_mconv, w_mq, w_mk, b_i, b_f, g_mh, w_out, ln1_g, ln1_b, w_up, w_fconv, b_fconv, w_down, ln2_g, ln2_b):
    B, S, D = x.shape
    assert D == D_MODEL
    T = B * S
    params = (w_in, g_qn, w_uq, g_kvn, w_ukv, w_pool, s_pool, w_mconv, b_mconv, w_mq, w_mk, b_i, b_f,
              g_mh, w_out, ln1_g, ln1_b, w_up, w_fconv, b_fconv, w_down, ln2_g, ln2_b)
    ctab, stab = _rope_tables(S)
    h = _input_ln(x.reshape(T, D), ln0_g, ln0_b, _tile(PROJ_TM, S))
    for l in range(DEPTH):
        lw = _prep_layer(l, *params)
        tm = _tile(PROJ_TM, S)
        qc, kc, vv, pm = _proj(h, ctab, stab, lw, S, tm, _tile(MIX_L, tm))
        a = _attention(qc, kc, vv, B, S, _tile(ATT_T, S // ATT_NSUB), ATT_NSUB)
        h = _block(h, a, pm, lw, S, _tile(FFN_TM, S // FFN_NSUB), FFN_NSUB, FFN_TF)
    return h.reshape(B, S, D)
```

```python
import functools

import jax
import jax.numpy as jnp
import numpy as np
from jax import lax
from jax.experimental import pallas as pl
from jax.experimental.pallas import tpu as pltpu

D_MODEL = 1024
DEPTH = 4
MLA_HEADS = 4
MLA_NOPE = 128
MLA_ROPE = 64
MLA_V = 128
Q_LORA = 384
KV_LORA = 256
ROPE_BASE = 10000.0
POOL_WINDOWS = (2, 4, 8, 16)
POOL_CH = 64
POOL_W = 256
M_HEADS = 4
M_HEAD_DIM = 64
M_W = 256
M_CONV = 4
D_FF = 2816
FFN_CONV = 3
LN_EPS = 1e-5
RMS_EPS = 1e-6
ALPHA = (2 * DEPTH) ** 0.25
ATT_SCALE = (MLA_NOPE + MLA_ROPE) ** -0.5
LOG2_E = 1.4426950408889634
Q_SCALE = ATT_SCALE * LOG2_E

LANES = 128
QK_W = 2 * LANES
Q_HEAD_COLS = 3 * LANES
POOL_HALO = 16
CONV_HALO = 8
REST_W = POOL_W + 2 * M_W
GATE_ROWS = 16
VMEM_LIMIT_BYTES = 56 * 1024 * 1024

PROJ_TM = 512
MIX_L = 256
ATT_T = 512
ATT_NSUB = 4
FFN_TM = 512
FFN_NSUB = 2
FFN_TF = 256

BF16 = jnp.bfloat16
F32 = jnp.float32


def _cparams(sem):
    return pltpu.CompilerParams(dimension_semantics=sem, vmem_limit_bytes=VMEM_LIMIT_BYTES)


def _layer_norm_rows(x, g, b):
    mu = jnp.mean(x, axis=-1, keepdims=True)
    d = x - mu
    var = jnp.mean(d * d, axis=-1, keepdims=True)
    return d * lax.rsqrt(var + LN_EPS) * g + b


def _ln_kernel(x_ref, g_ref, b_ref, o_ref):
    o_ref[...] = _layer_norm_rows(x_ref[...], g_ref[...], b_ref[...])


def _input_ln(x2, g, b, tm):
    T, D = x2.shape
    return pl.pallas_call(
        _ln_kernel,
        grid=(T // tm,),
        in_specs=[pl.BlockSpec((tm, D), lambda i: (i, 0)),
                  pl.BlockSpec((1, D), lambda i: (0, 0)),
                  pl.BlockSpec((1, D), lambda i: (0, 0))],
        out_specs=pl.BlockSpec((tm, D), lambda i: (i, 0)),
        out_shape=jax.ShapeDtypeStruct((T, D), F32),
        compiler_params=_cparams(("arbitrary",)),
        name="input_ln",
    )(x2, g.reshape(1, D), b.reshape(1, D))


def _proj_kernel(h_ref, ct_ref, st_ref, wa_ref, wr_ref, wvt_ref, wgt_ref, gq_ref, gkv_ref,
                 wq_ref, wkv_ref,
                 wpool_ref, spool_ref, wconv_ref, bconv_ref, wmq_ref, wmk_ref, gbias_ref, gbiasr_ref, gmh_ref,
                 q_ref, k_ref, v_ref, pm_ref,
                 xp_buf, xc_buf, c_st, n_st, m_st, rest_c, vt_c, gt_c, gc_c, *, nsb, L):
    i = pl.program_id(0)
    tm = h_ref.shape[0]
    jm = jnp.maximum(i - 1, 0)

    @pl.when(i == 0)
    def _():
        rest_c[...] = jnp.zeros_like(rest_c)
        vt_c[...] = jnp.zeros_like(vt_c)
        gt_c[...] = jnp.zeros_like(gt_c)
        gc_c[...] = jnp.zeros_like(gc_c)

    @pl.when((jm % nsb) == 0)
    def _():
        xp_buf[0:POOL_HALO, :] = jnp.zeros((POOL_HALO, POOL_W), F32)
        xc_buf[0:CONV_HALO, :] = jnp.zeros((CONV_HALO, M_W), F32)
        c_st[...] = jnp.zeros_like(c_st)
        n_st[...] = jnp.zeros_like(n_st)
        m_st[...] = jnp.zeros_like(m_st)

    rest_p = rest_c[...]
    vt_p = vt_c[...]
    gt_p = gt_c[...]
    gc_p = gc_c[...]

    hb = h_ref[...].astype(BF16)
    ct = ct_ref[...]
    st = st_ref[...]
    nt = (((1,), (1,)), ((), ()))
    vals = {}

    def mix_inputs_a():
        vals["rest"] = jnp.dot(hb, wr_ref[...], preferred_element_type=F32)

    def mix_inputs_b():
        vals["vt"] = lax.dot_general(wvt_ref[...], hb, nt, preferred_element_type=F32).astype(BF16)
        vals["gt"] = lax.dot_general(wgt_ref[...], hb, nt, preferred_element_type=F32)

    def mla_q_latent():
        pq = jnp.dot(hb, wa_ref[:, :Q_LORA + LANES], preferred_element_type=F32)
        vals["gc"] = pq[:, Q_LORA:]
        cq = pq[:, :Q_LORA]
        nq = cq * lax.rsqrt(jnp.mean(cq * cq, axis=-1, keepdims=True) + RMS_EPS) * gq_ref[...]
        vals["nq"] = nq.astype(BF16)

    def mla_q_heads(h0, h1):
        def emit():
            for hd in range(h0, h1):
                base = hd * Q_HEAD_COLS
                y = jnp.dot(vals["nq"], wq_ref[:, base:base + Q_HEAD_COLS], preferred_element_type=F32)
                q_ref[:, hd * QK_W:hd * QK_W + LANES] = (y[:, :LANES] * Q_SCALE).astype(BF16)
                rot = y[:, LANES:2 * LANES] * ct + y[:, 2 * LANES:] * st
                q_ref[:, hd * QK_W + LANES:(hd + 1) * QK_W] = (rot * Q_SCALE).astype(BF16)
        return emit

    def mla_kv_latent():
        pkv = jnp.dot(hb, wa_ref[:, Q_LORA + LANES:], preferred_element_type=F32)
        ckv = pkv[:, :KV_LORA]
        nkv = ckv * lax.rsqrt(jnp.mean(ckv * ckv, axis=-1, keepdims=True) + RMS_EPS) * gkv_ref[...]
        vals["nkv"] = nkv.astype(BF16)
        vals["kpe"] = (pkv[:, KV_LORA:KV_LORA + LANES] * ct + pkv[:, KV_LORA + LANES:] * st).astype(BF16)

    def mla_k():
        yk = jnp.dot(vals["nkv"], wkv_ref[:, :MLA_HEADS * MLA_NOPE], preferred_element_type=F32)
        for hd in range(MLA_HEADS):
            k_ref[:, hd * QK_W:hd * QK_W + LANES] = yk[:, hd * LANES:(hd + 1) * LANES].astype(BF16)
            k_ref[:, hd * QK_W + LANES:(hd + 1) * QK_W] = vals["kpe"]

    def mla_v():
        yv = jnp.dot(vals["nkv"], wkv_ref[:, MLA_HEADS * MLA_NOPE:], preferred_element_type=F32)
        ones = jnp.ones((tm, MLA_V), BF16)
        for hd in range(MLA_HEADS):
            v_ref[:, hd * QK_W:hd * QK_W + MLA_V] = yv[:, hd * MLA_V:(hd + 1) * MLA_V].astype(BF16)
            v_ref[:, hd * QK_W + MLA_V:(hd + 1) * QK_W] = ones

    fillers = [mix_inputs_a, mix_inputs_b, mla_q_latent, mla_q_heads(0, 2), mla_q_heads(2, 4),
               mla_kv_latent, mla_k, mla_v]
    for ci in range(tm // L):
        rows = slice(ci * L, (ci + 1) * L)
        _mix_chunk(rest_p[rows, 0:POOL_W], rest_p[rows, POOL_W:POOL_W + M_W], rest_p[rows, POOL_W + M_W:REST_W],
                   vt_p[:, rows], gt_p[:, rows], gc_p[rows, :], (jm % nsb) * tm + ci * L, rows, pm_ref,
                   wpool_ref, spool_ref, wconv_ref, bconv_ref, wmq_ref, wmk_ref, gbias_ref, gbiasr_ref, gmh_ref,
                   xp_buf, xc_buf, c_st, n_st, m_st, fillers)
    while fillers:
        fillers.pop(0)()
    rest_c[...] = vals["rest"]
    vt_c[...] = vals["vt"]
    gt_c[...] = vals["gt"]
    gc_c[...] = vals["gc"]


def _proj(h, ctab, stab, lw, S, tm, L):
    T = h.shape[0]
    nsb = S // tm
    full = lambda a: pl.BlockSpec(a.shape, lambda i: (0,) * a.ndim)
    weights = [lw["wa"], lw["wr"], lw["wvt"], lw["wgt"], lw["gq"], lw["gkv"], lw["wq"], lw["wkv"],
               lw["wpool"], lw["spool"], lw["wconv"], lw["bconv"], lw["wmq"], lw["wmk"], lw["gbias"],
               lw["gbias_row"], lw["gmh"]]
    HW = MLA_HEADS * QK_W
    nt_ = T // tm
    cur = lambda i: jnp.minimum(i, nt_ - 1)
    prev = lambda i: jnp.maximum(i - 1, 0)
    return pl.pallas_call(
        functools.partial(_proj_kernel, nsb=nsb, L=L),
        grid=(nt_ + 1,),
        in_specs=[pl.BlockSpec((tm, D_MODEL), lambda i: (cur(i), 0)),
                  pl.BlockSpec((tm, LANES), lambda i: (cur(i) % nsb, 0)),
                  pl.BlockSpec((tm, LANES), lambda i: (cur(i) % nsb, 0))] + [full(w) for w in weights],
        out_specs=[pl.BlockSpec((tm, HW), lambda i: (cur(i), 0)),
                   pl.BlockSpec((tm, HW), lambda i: (cur(i), 0)),
                   pl.BlockSpec((tm, HW), lambda i: (cur(i), 0)),
                   pl.BlockSpec((tm, POOL_W + M_W), lambda i: (prev(i), 0))],
        out_shape=[jax.ShapeDtypeStruct((T, HW), BF16),
                   jax.ShapeDtypeStruct((T, HW), BF16),
                   jax.ShapeDtypeStruct((T, HW), BF16),
                   jax.ShapeDtypeStruct((T, POOL_W + M_W), BF16)],
        scratch_shapes=[pltpu.VMEM((POOL_HALO + L, POOL_W), F32),
                        pltpu.VMEM((CONV_HALO + L, M_W), F32),
                        pltpu.VMEM((M_W, M_W), F32),
                        pltpu.VMEM((1, M_W), F32),
                        pltpu.VMEM((2 * M_HEADS, LANES), F32),
                        pltpu.VMEM((tm, REST_W), F32),
                        pltpu.VMEM((M_W, tm), BF16),
                        pltpu.VMEM((GATE_ROWS, tm), F32),
                        pltpu.VMEM((tm, LANES), F32)],
        compiler_params=_cparams(("arbitrary",)),
        name="proj_mix",
    )(h, ctab, stab, *weights)


def _split3(x):
    hi = x.astype(BF16)
    r = x - hi.astype(F32)
    mid = r.astype(BF16)
    lo = (r - mid.astype(F32)).astype(BF16)
    return hi, mid, lo


def _mix_chunk(xp, xqk, xo, vt, gt_raw, gc_raw, tpos0, rows, o_ref,
               wpool_ref, spool_ref, wconv_ref, bconv_ref, wmq_ref, wmk_ref, gbias_ref, gbiasr_ref, gmh_ref,
               xp_buf, xc_buf, c_st, n_st, m_st, fillers):
    L = xp.shape[0]
    H = M_HEADS

    xp_buf[POOL_HALO:POOL_HALO + L, :] = xp
    lane = lax.broadcasted_iota(jnp.int32, (L, POOL_W), 1)
    grp = lane // POOL_CH
    acc = xp_buf[...]
    wsum = jnp.zeros((L, POOL_W), F32)
    for gi, w in enumerate(POOL_WINDOWS):
        assert w == 2 ** (gi + 1)
        acc = acc + pltpu.roll(acc, w // 2, axis=0)
        wsum = jnp.where(grp == gi, acc[POOL_HALO:, :], wsum)
    win = jnp.where(grp == 0, POOL_WINDOWS[0],
                    jnp.where(grp == 1, POOL_WINDOWS[1],
                              jnp.where(grp == 2, POOL_WINDOWS[2], POOL_WINDOWS[3])))
    tpos = tpos0 + lax.broadcasted_iota(jnp.int32, (L, POOL_W), 0)
    cnt = jnp.minimum(tpos + 1, win).astype(F32)
    y = wsum / cnt - xp
    pooled = jnp.dot(y.astype(BF16), wpool_ref[...], preferred_element_type=F32) * spool_ref[...]
    o_ref[rows, 0:POOL_W] = pooled.astype(BF16)
    xp_buf[0:POOL_HALO, :] = xp[L - POOL_HALO:, :]

    def next_filler():
        if fillers:
            fillers.pop(0)()

    next_filler()
    xc_buf[CONV_HALO:CONV_HALO + L, :] = xqk
    conv = xqk * wconv_ref[M_CONV - 1:M_CONV, :] + bconv_ref[...]
    for k in range(1, M_CONV):
        conv = conv + xc_buf[CONV_HALO - k:CONV_HALO - k + L, :] * wconv_ref[M_CONV - 1 - k:M_CONV - k, :]
    xc_buf[0:CONV_HALO, :] = xqk[L - CONV_HALO:, :]
    cs = (conv * jax.nn.sigmoid(conv)).astype(BF16)
    q = jnp.dot(cs, wmq_ref[...], preferred_element_type=F32)
    kk = jnp.dot(cs, wmk_ref[...], preferred_element_type=F32) * (M_HEAD_DIM ** -0.5)
    next_filler()

    tri_r = lax.broadcasted_iota(jnp.int32, (L, L), 0)
    tri_c = lax.broadcasted_iota(jnp.int32, (L, L), 1)
    causal = tri_r <= tri_c

    def cumsum3(lhs_parts, rhs_parts):
        out = None
        for lp, rp in zip(lhs_parts, rhs_parts):
            d = jnp.dot(lp, rp, preferred_element_type=F32)
            out = d if out is None else out + d
        return out

    upper = causal.astype(BF16)
    lower = (tri_c <= tri_r).astype(BF16)
    gt = gt_raw + gbias_ref[...]
    li_r = gt[0:H, :]
    lf_r = jax.nn.log_sigmoid(gt)
    b_r = cumsum3(_split3(lf_r), [upper] * 3)[H:2 * H, :]
    gc = gc_raw + gbiasr_ref[...]
    li_c = gc
    lf_c = jax.nn.log_sigmoid(pltpu.roll(gc, LANES // 2, axis=1))
    b_c = cumsum3([lower] * 3, _split3(lf_c))
    a_c = li_c - b_c

    lane_r = lax.broadcasted_iota(jnp.int32, (1, LANES), 1)
    m_prev = m_st[...]
    e_blocks, g_rows, eneg_rows, mt_last, b_last = [], [], [], [], []
    for hh in range(H):
        a_col = jnp.broadcast_to(a_c[:, hh:hh + 1], (L, L))
        cm = jnp.max(jnp.where(causal, a_col, -jnp.inf), axis=0, keepdims=True)
        mp = jnp.concatenate([m_prev[hh:hh + 1, :]] * (L // LANES), axis=1)
        m_row = jnp.maximum(mp, cm)
        e_blocks.append(jnp.exp(jnp.where(causal, a_col - m_row, -jnp.inf)))
        g_rows.append(jnp.exp(mp - m_row))
        mt = b_r[hh:hh + 1, :] + m_row
        eneg_rows.append(jnp.exp(-mt))
        mt_last.append(mt[:, L - 1:L])
        b_last.append(b_r[hh:hh + 1, L - 1:L])
    ET = jnp.concatenate(e_blocks, axis=0)
    next_filler()

    hlane = lax.broadcasted_iota(jnp.int32, (L, M_W), 1) // M_HEAD_DIM
    kb = kk.astype(BF16)
    qb = q.astype(BF16)
    zero_b = jnp.zeros((L, M_W), BF16)
    k4 = jnp.concatenate([jnp.where(hlane == hh, kb, zero_b) for hh in range(H)], axis=0)
    nt = (((1,), (1,)), ((), ()))
    scT = lax.dot_general(k4, qb, nt, preferred_element_type=F32) * ET
    den_rows = [jnp.sum(scT[hh * L:(hh + 1) * L, :], axis=0, keepdims=True) for hh in range(H)]
    hrow =lax.broadcasted_iota(jnp.int32, (M_W, L), 0) // M_HEAD_DIM
    zero_v = jnp.zeros((M_W, L), BF16)
    vt4 = jnp.concatenate([jnp.where(hrow == hh, vt, zero_v) for hh in range(H)], axis=1)
    numT = jnp.dot(vt4, scT.astype(BF16), preferred_element_type=F32)
    ct_prev = c_st[...]
    interT = lax.dot_general(ct_prev.astype(BF16), qb, nt, preferred_element_type=F32)
    n_prev = n_st[...]
    nrow = lax.broadcasted_iota(jnp.int32, (GATE_ROWS, M_W), 0)
    nlane = lax.broadcasted_iota(jnp.int32, (GATE_ROWS, M_W), 1) // M_HEAD_DIM
    n4 = jnp.where(nrow == nlane, jnp.broadcast_to(n_prev, (GATE_ROWS, M_W)), 0.0).astype(BF16)
    qn = lax.dot_general(n4, qb, nt, preferred_element_type=F32)

    y_blocks = []
    for hh in range(H):
        erows = slice(hh * M_HEAD_DIM, (hh + 1) * M_HEAD_DIM)
        den = g_rows[hh] * qn[hh:hh + 1, :] + den_rows[hh]
        rinv = 1.0 / jnp.maximum(jnp.abs(den), eneg_rows[hh])
        hblk = (g_rows[hh] * interT[erows, :] + numT[erows, :]) * rinv
        mu = jnp.mean(hblk, axis=0, keepdims=True)
        dd = hblk - mu
        var = jnp.mean(dd * dd, axis=0, keepdims=True)
        y_blocks.append(dd * lax.rsqrt(var + LN_EPS))
    yn = jnp.concatenate(y_blocks, axis=0).T
    o_ref[rows, POOL_W:POOL_W + M_W] = (jax.nn.sigmoid(xo) * (yn * gmh_ref[...])).astype(BF16)
    next_filler()

    lane_h = lax.broadcasted_iota(jnp.int32, (1, M_W), 1) // M_HEAD_DIM
    hv = jnp.zeros((1, LANES), F32)
    dec_row = jnp.zeros((1, M_W), F32)
    for hh in range(H):
        m_new = mt_last[hh]
        hv = jnp.where(lane_r == hh, b_last[hh] - m_new, hv)
        decay = jnp.exp(b_last[hh] + m_prev[hh:hh + 1, 0:1] - m_new)
        dec_row = jnp.where(lane_h == hh, decay, dec_row)
        m_st[hh:hh + 1, :] = jnp.broadcast_to(m_new, (1, LANES))
    w_c = jnp.exp(a_c + hv)
    wexp = jnp.zeros((L, M_W), F32)
    for hh in range(H):
        wexp = jnp.where(hlane == hh, jnp.broadcast_to(w_c[:, hh:hh + 1], (L, M_W)), wexp)
    kw = kk * wexp
    upd = jnp.dot(vt, kw.astype(BF16), preferred_element_type=F32)
    bd = (lax.broadcasted_iota(jnp.int32, (M_W, M_W), 0) // M_HEAD_DIM
          == lax.broadcasted_iota(jnp.int32, (M_W, M_W), 1) // M_HEAD_DIM)
    c_st[...] = jnp.where(bd, dec_row * ct_prev + upd, 0.0)
    n_st[...] = dec_row * n_prev + jnp.sum(kw, axis=0, keepdims=True)


def _attn_kernel(q_ref, k_ref, v_ref, o_ref, m_sc, acc_sc, *, t, nsub):
    qi = pl.program_id(2)
    for r in range(nsub):
        m_sc[r] = jnp.full((t, LANES), -jnp.inf, F32)
        acc_sc[r] = jnp.zeros((t, 2 * MLA_V), F32)

    def chain_step(r, j, masked):
        start = pl.multiple_of(j * t, t)
        q = q_ref[r * t:(r + 1) * t, :]
        k = k_ref[pl.ds(start, t), :]
        v = v_ref[pl.ds(start, t), :]
        s = lax.dot_general(q, k, (((1,), (1,)), ((), ())), preferred_element_type=F32)
        if masked:
            ri = lax.broadcasted_iota(jnp.int32, (t, t), 0)
            ci = lax.broadcasted_iota(jnp.int32, (t, t), 1)
            s = jnp.where(ci <= ri, s, -jnp.inf)
        m_old = m_sc[r]
        m_new = jnp.maximum(m_old, jnp.max(s, axis=-1, keepdims=True))
        p = jnp.exp2(s - jnp.concatenate([m_new] * (t // LANES), axis=1))
        alpha = jnp.exp2(m_old - m_new)
        pv = jnp.dot(p.astype(BF16), v, preferred_element_type=F32)
        acc_sc[r] = jnp.concatenate([alpha, alpha], axis=1) * acc_sc[r] + pv
        m_sc[r] = m_new

    def body(i, carry):
        for jj in range(nsub):
            for r in range(nsub):
                chain_step(r, nsub * i + jj, False)
        return carry

    lax.fori_loop(0, qi, body, 0)
    for kt in range(nsub):
        for r in range(kt, nsub):
            chain_step(r, nsub * qi + kt, r == kt)
    for r in range(nsub):
        acc = acc_sc[r]
        o_ref[r * t:(r + 1) * t, :] = (acc[:, :MLA_V] / acc[:, MLA_V:]).astype(o_ref.dtype)


def _attention(qc, kc, vv, B, S, t, nsub):
    T = qc.shape[0]
    tq = t * nsub
    nq = S // tq
    return pl.pallas_call(
        functools.partial(_attn_kernel, t=t, nsub=nsub),
        grid=(B, MLA_HEADS, nq),
        in_specs=[pl.BlockSpec((tq, QK_W), lambda b, h, i: (b * nq + i, h)),
                  pl.BlockSpec((S, QK_W), lambda b, h, i: (b, h)),
                  pl.BlockSpec((S, 2 * MLA_V), lambda b, h, i: (b, h))],
        out_specs=pl.BlockSpec((tq, MLA_V), lambda b, h, i: (b * nq + i, h)),
        out_shape=jax.ShapeDtypeStruct((T, MLA_HEADS * MLA_V), BF16),
        scratch_shapes=[pltpu.VMEM((nsub, t, LANES), F32), pltpu.VMEM((nsub, t, 2 * MLA_V), F32)],
        compiler_params=_cparams(("arbitrary", "arbitrary", "arbitrary")),
        name="mla_attention",
    )(qc, kc, vv)


def _block_kernel(h_ref, a_ref, pm_ref, woa_ref, wopm_ref, g1_ref, b1_ref,
                  wa_ref, wg_ref, wfc_ref, bfc_ref, wd_ref, g2_ref, b2_ref, o_ref,
                  tail_sc, u_buf, *, ts, nsub, tf, nsb):
    i = pl.program_id(0)
    nchunk = D_FF // tf

    @pl.when((i % nsb) == 0)
    def _():
        tail_sc[...] = jnp.zeros((CONV_HALO, D_FF), F32)

    row8 = lax.broadcasted_iota(jnp.int32, (CONV_HALO, tf), 0)
    h1s, hbs = [], []
    for r in range(nsub):
        rows = slice(r * ts, (r + 1) * ts)
        mix = (jnp.dot(a_ref[rows, :], woa_ref[...], preferred_element_type=F32)
               + jnp.dot(pm_ref[rows, :], wopm_ref[...], preferred_element_type=F32))
        h1 = _layer_norm_rows(ALPHA * h_ref[rows, :] + mix, g1_ref[...], b1_ref[...])
        h1s.append(h1)
        hbs.append(h1.astype(BF16))

    def up(r, c):
        cols = slice(c * tf, (c + 1) * tf)
        g = jnp.dot(hbs[r], wg_ref[:, cols], preferred_element_type=F32)
        a = jnp.dot(hbs[r], wa_ref[:, cols], preferred_element_type=F32)
        return g, a

    def shifted(g, prev8, k):
        body = pltpu.roll(g, k, axis=0)
        top = jnp.where(row8 < k, pltpu.roll(prev8, k, axis=0), body[0:CONV_HALO, :])
        return jnp.concatenate([top, body[CONV_HALO:, :]], axis=0)

    order = [(r, c) for c in range(nchunk) for r in range(nsub)]
    nxt = up(*order[0])
    for n, (r, c) in enumerate(order):
        cols = slice(c * tf, (c + 1) * tf)
        g, a = nxt
        if n + 1 < len(order):
            nxt = up(*order[n + 1])
        prev8 = tail_sc[:, cols]
        tail_sc[:, cols] = g[ts - CONV_HALO:, :]
        conv = g * wfc_ref[FFN_CONV - 1:FFN_CONV, cols] + bfc_ref[:, cols]
        for k in range(1, FFN_CONV):
            conv = conv + shifted(g, prev8, k) * wfc_ref[FFN_CONV - 1 - k:FFN_CONV - k, cols]
        u_buf[r, :, cols] = (conv * jax.nn.sigmoid(conv) * a).astype(BF16)
    for r in range(nsub):
        f = jnp.dot(u_buf[r], wd_ref[...], preferred_element_type=F32)
        o_ref[r * ts:(r + 1) * ts, :] = _layer_norm_rows(ALPHA * h1s[r] + f, g2_ref[...], b2_ref[...])


def _block(h, a, pm, lw, S, ts, nsub, tf):
    T = h.shape[0]
    tm = ts * nsub
    nsb = S // tm
    const = lambda x: pl.BlockSpec(x.shape, lambda i: (0,) * x.ndim, pipeline_mode=pl.Buffered(1))
    weights = [lw["woa"], lw["wopm"], lw["ln1g"], lw["ln1b"], lw["wup_a"], lw["wup_g"], lw["wfc"], lw["bfc"],
               lw["wd"], lw["ln2g"], lw["ln2b"]]
    return pl.pallas_call(
        functools.partial(_block_kernel, ts=ts, nsub=nsub, tf=tf, nsb=nsb),
        grid=(T // tm,),
        in_specs=[pl.BlockSpec((tm, D_MODEL), lambda i: (i, 0)),
                  pl.BlockSpec((tm, a.shape[1]), lambda i: (i, 0)),
                  pl.BlockSpec((tm, pm.shape[1]), lambda i: (i, 0))] + [const(w) for w in weights],
        out_specs=pl.BlockSpec((tm, D_MODEL), lambda i: (i, 0)),
        out_shape=jax.ShapeDtypeStruct((T, D_MODEL), F32),
        scratch_shapes=[pltpu.VMEM((CONV_HALO, D_FF), F32),
                        pltpu.VMEM((nsub, ts, D_FF), BF16)],
        compiler_params=_cparams(("arbitrary",)),
        name="outproj_ffn",
    )(h, a, pm, *weights)


def _block_diag(w):
    G, C, _ = w.shape
    out = jnp.zeros((G * C, G * C), w.dtype)
    for g in range(G):
        out = out.at[g * C:(g + 1) * C, g * C:(g + 1) * C].set(w[g])
    return out


def _prep_layer(l, w_in, g_qn, w_uq, g_kvn, w_ukv, w_pool, s_pool, w_mconv, b_mconv, w_mq, w_mk,
                b_i, b_f, g_mh, w_out, ln1_g, ln1_b, w_up, w_fconv, b_fconv, w_down, ln2_g, ln2_b):
    half = MLA_ROPE // 2
    wi = w_in[l]
    o_kr = Q_LORA + KV_LORA
    o_rest = o_kr + MLA_ROPE
    o_gate = o_rest + 4 * 256
    kr = wi[:, o_kr:o_kr + MLA_ROPE]
    zpad = jnp.zeros((D_MODEL, LANES - MLA_ROPE), F32)
    kr_sw = jnp.concatenate([-kr[:, half:], kr[:, :half]], axis=1)
    wgate = wi[:, o_gate:o_gate + 2 * M_HEADS]
    zg = jnp.zeros((D_MODEL, LANES // 2 - M_HEADS), F32)
    wgc = jnp.concatenate([wgate[:, :M_HEADS], zg, wgate[:, M_HEADS:], zg], axis=1)
    wa = jnp.concatenate([wi[:, :Q_LORA], wgc, wi[:, Q_LORA:o_kr], kr, zpad, kr_sw, zpad], axis=1)
    o_xv = o_rest + POOL_W + M_W
    wr = jnp.concatenate([wi[:, o_rest:o_xv], wi[:, o_xv + M_W:o_gate]], axis=1)
    wvt = wi[:, o_xv:o_xv + M_W].T
    wgt = jnp.concatenate([wgate.T, jnp.zeros((GATE_ROWS - 2 * M_HEADS, D_MODEL), F32)], axis=0)
    zb = jnp.zeros((LANES // 2 - M_HEADS,), F32)
    gbias_row = jnp.concatenate([b_i[l], zb, b_f[l], zb]).reshape(1, LANES)

    wuq = w_uq[l].reshape(Q_LORA, MLA_HEADS, MLA_NOPE + MLA_ROPE)
    zq = jnp.zeros((Q_LORA, MLA_HEADS, LANES - MLA_ROPE), F32)
    rope = wuq[:, :, MLA_NOPE:]
    wq = jnp.concatenate([wuq, zq, -rope[:, :, half:], rope[:, :, :half], zq],
                         axis=2).reshape(Q_LORA, MLA_HEADS * Q_HEAD_COLS)

    wukv = w_ukv[l].reshape(KV_LORA, MLA_HEADS, MLA_NOPE + MLA_V)
    wkv = jnp.concatenate([wukv[:, :, :MLA_NOPE].reshape(KV_LORA, -1),
                           wukv[:, :, MLA_NOPE:].reshape(KV_LORA, -1)], axis=1)

    gbias = jnp.concatenate([b_i[l], b_f[l], jnp.zeros((GATE_ROWS - 2 * M_HEADS,), F32)]).reshape(GATE_ROWS, 1)
    wo = w_out[l]
    return dict(
        wa=wa.astype(BF16), wr=wr.astype(BF16), wvt=wvt.astype(BF16), wgt=wgt.astype(BF16),
        gbias_row=gbias_row,
        gq=g_qn[l].reshape(1, Q_LORA), gkv=g_kvn[l].reshape(1, KV_LORA),
        wq=wq.astype(BF16), wkv=wkv.astype(BF16),
        wpool=_block_diag(w_pool[l]).astype(BF16), spool=s_pool[l].reshape(1, POOL_W),
        wconv=w_mconv[l], bconv=b_mconv[l].reshape(1, M_W),
        wmq=_block_diag(w_mq[l]).astype(BF16), wmk=_block_diag(w_mk[l]).astype(BF16),
        gbias=gbias, gmh=g_mh[l].reshape(1, M_W),
        woa=wo[:MLA_HEADS * MLA_V].astype(BF16), wopm=wo[MLA_HEADS * MLA_V:].astype(BF16),
        ln1g=ln1_g[l].reshape(1, D_MODEL), ln1b=ln1_b[l].reshape(1, D_MODEL),
        wup_a=w_up[l][:, :D_FF].astype(BF16), wup_g=w_up[l][:, D_FF:].astype(BF16),
        wfc=w_fconv[l], bfc=b_fconv[l].reshape(1, D_FF), wd=w_down[l].astype(BF16),
        ln2g=ln2_g[l].reshape(1, D_MODEL), ln2b=ln2_b[l].reshape(1, D_MODEL),
    )


def _rope_tables(S):
    inv = 1.0 / (ROPE_BASE ** (jnp.arange(0, MLA_ROPE, 2, dtype=F32) / MLA_ROPE))
    ang = jnp.arange(S, dtype=F32)[:, None] * inv[None, :]
    z = jnp.zeros((S, LANES - MLA_ROPE), F32)
    ctab = jnp.concatenate([jnp.cos(ang), jnp.cos(ang), z], axis=1)
    stab = jnp.concatenate([jnp.sin(ang), jnp.sin(ang), z], axis=1)
    return ctab, stab


def _tile(pref, n):
    t = min(pref, n)
    assert n % t == 0, (pref, n)
    return t


def kernel(x, ln0_g, ln0_b, w_in, g_qn, w_uq, g_kvn, w_ukv, w_pool, s_pool, w_mconv, b_mconv, w_mq, w_mk, b_i, b_f, g_mh, w_out, ln1_g, ln1_b, w_up, w_fconv, b_fconv, w_down, ln2_g, ln2_b):
    B, S, D = x.shape
    assert D == D_MODEL
    T = B * S
    params = (w_in, g_qn, w_uq, g_kvn, w_ukv, w_pool, s_pool, w_mconv, b_mconv, w_mq, w_mk, b_i, b_f,
              g_mh, w_out, ln1_g, ln1_b, w_up, w_fconv, b_fconv, w_down, ln2_g, ln2_b)
    ctab, stab = _rope_tables(S)
    h = _input_ln(x.reshape(T, D), ln0_g, ln0_b, _tile(PROJ_TM, S))
    for l in range(DEPTH):
        lw = _prep_layer(l, *params)
        tm = _tile(PROJ_TM, S)
        qc, kc, vv, pm = _proj(h, ctab, stab, lw, S, tm, _tile(MIX_L, tm))
        a = _attention(qc, kc, vv, B, S, _tile(ATT_T, S // ATT_NSUB), ATT_NSUB)
        h = _block(h, a, pm, lw, S, _tile(FFN_TM, S // FFN_NSUB), FFN_NSUB, FFN_TF)
    return h.reshape(B, S, D)
```

```python
import functools

import jax
import jax.numpy as jnp
import numpy as np
from jax import lax
from jax.experimental import pallas as pl
from jax.experimental.pallas import tpu as pltpu

D_MODEL = 1024
DEPTH = 4
MLA_HEADS = 4
MLA_NOPE = 128
MLA_ROPE = 64
MLA_V = 128
Q_LORA = 384
KV_LORA = 256
ROPE_BASE = 10000.0
POOL_WINDOWS = (2, 4, 8, 16)
POOL_CH = 64
POOL_W = 256
M_HEADS = 4
M_HEAD_DIM = 64
M_W = 256
M_CONV = 4
D_FF = 2816
FFN_CONV = 3
LN_EPS = 1e-5
RMS_EPS = 1e-6
ALPHA = (2 * DEPTH) ** 0.25
ATT_SCALE = (MLA_NOPE + MLA_ROPE) ** -0.5
LOG2_E = 1.4426950408889634
Q_SCALE = ATT_SCALE * LOG2_E

LANES = 128
QK_W = 2 * LANES
Q_HEAD_COLS = 3 * LANES
POOL_HALO = 16
CONV_HALO = 8
REST_W = POOL_W + 2 * M_W
GATE_ROWS = 16
VMEM_LIMIT_BYTES = 56 * 1024 * 1024

PROJ_TM = 512
MIX_L = 256
ATT_T = 512
ATT_NSUB = 4
FFN_TM = 512
FFN_NSUB = 2
FFN_TF = 256

BF16 = jnp.bfloat16
F32 = jnp.float32


def _cparams(sem):
    return pltpu.CompilerParams(dimension_semantics=sem, vmem_limit_bytes=VMEM_LIMIT_BYTES)


def _layer_norm_rows(x, g, b):
    mu = jnp.mean(x, axis=-1, keepdims=True)
    d = x - mu
    var = jnp.mean(d * d, axis=-1, keepdims=True)
    return d * lax.rsqrt(var + LN_EPS) * g + b


def _proj_kernel(h_ref, ct_ref, st_ref, g0_ref, b0_ref, wa_ref, wr_ref, wvt_ref, wgt_ref, gq_ref, gkv_ref,
                 wq_ref, wkv_ref,
                 wpool_ref, spool_ref, wconv_ref, bconv_ref, wmq_ref, wmk_ref, gbias_ref, gbiasr_ref, gmh_ref,
                 q_ref, k_ref, v_ref, pm_ref,
                 xp_buf, xc_buf, c_st, n_st, m_st, rest_c, vt_c, gt_c, gc_c, *, nsb, L, ln_in):
    i = pl.program_id(0)
    tm = h_ref.shape[0]
    jm = jnp.maximum(i - 1, 0)

    @pl.when(i == 0)
    def _():
        rest_c[...] = jnp.zeros_like(rest_c)
        vt_c[...] = jnp.zeros_like(vt_c)
        gt_c[...] = jnp.zeros_like(gt_c)
        gc_c[...] = jnp.zeros_like(gc_c)

    @pl.when((jm % nsb) == 0)
    def _():
        xp_buf[0:POOL_HALO, :] = jnp.zeros((POOL_HALO, POOL_W), F32)
        xc_buf[0:CONV_HALO, :] = jnp.zeros((CONV_HALO, M_W), F32)
        c_st[...] = jnp.zeros_like(c_st)
        n_st[...] = jnp.zeros_like(n_st)
        m_st[...] = jnp.zeros_like(m_st)

    rest_p = rest_c[...]
    vt_p = vt_c[...]
    gt_p = gt_c[...]
    gc_p = gc_c[...]

    h = h_ref[...]
    if ln_in:
        h = _layer_norm_rows(h, g0_ref[...], b0_ref[...])
    hb = h.astype(BF16)
    ct = ct_ref[...]
    st = st_ref[...]
    nt = (((1,), (1,)), ((), ()))
    vals = {}

    def mix_inputs_a():
        vals["rest"] = jnp.dot(hb, wr_ref[...], preferred_element_type=F32)

    def mix_inputs_b():
        vals["vt"] = lax.dot_general(wvt_ref[...], hb, nt, preferred_element_type=F32).astype(BF16)
        vals["gt"] = lax.dot_general(wgt_ref[...], hb, nt, preferred_element_type=F32)

    def mla_q_latent():
        pq = jnp.dot(hb, wa_ref[:, :Q_LORA + LANES], preferred_element_type=F32)
        vals["gc"] = pq[:, Q_LORA:]
        cq = pq[:, :Q_LORA]
        nq = cq * lax.rsqrt(jnp.mean(cq * cq, axis=-1, keepdims=True) + RMS_EPS) * gq_ref[...]
        vals["nq"] = nq.astype(BF16)

    def mla_q_heads(h0, h1):
        def emit():
            for hd in range(h0, h1):
                base = hd * Q_HEAD_COLS
                y = jnp.dot(vals["nq"], wq_ref[:, base:base + Q_HEAD_COLS], preferred_element_type=F32)
                q_ref[:, hd * QK_W:hd * QK_W + LANES] = (y[:, :LANES] * Q_SCALE).astype(BF16)
                rot = y[:, LANES:2 * LANES] * ct + y[:, 2 * LANES:] * st
                q_ref[:, hd * QK_W + LANES:(hd + 1) * QK_W] = (rot * Q_SCALE).astype(BF16)
        return emit

    def mla_kv_latent():
        pkv = jnp.dot(hb, wa_ref[:, Q_LORA + LANES:], preferred_element_type=F32)
        ckv = pkv[:, :KV_LORA]
        nkv = ckv * lax.rsqrt(jnp.mean(ckv * ckv, axis=-1, keepdims=True) + RMS_EPS) * gkv_ref[...]
        vals["nkv"] = nkv.astype(BF16)
        vals["kpe"] = (pkv[:, KV_LORA:KV_LORA + LANES] * ct + pkv[:, KV_LORA + LANES:] * st).astype(BF16)

    def mla_k():
        yk = jnp.dot(vals["nkv"], wkv_ref[:, :MLA_HEADS * MLA_NOPE], preferred_element_type=F32)
        for hd in range(MLA_HEADS):
            k_ref[:, hd * QK_W:hd * QK_W + LANES] = yk[:, hd * LANES:(hd + 1) * LANES].astype(BF16)
            k_ref[:, hd * QK_W + LANES:(hd + 1) * QK_W] = vals["kpe"]

    def mla_v():
        yv = jnp.dot(vals["nkv"], wkv_ref[:, MLA_HEADS * MLA_NOPE:], preferred_element_type=F32)
        ones = jnp.ones((tm, MLA_V), BF16)
        for hd in range(MLA_HEADS):
            v_ref[:, hd * QK_W:hd * QK_W + MLA_V] = yv[:, hd * MLA_V:(hd + 1) * MLA_V].astype(BF16)
            v_ref[:, hd * QK_W + MLA_V:(hd + 1) * QK_W] = ones

    fillers = [mix_inputs_a, mix_inputs_b, mla_q_latent, mla_q_heads(0, 2), mla_q_heads(2, 4),
               mla_kv_latent, mla_k, mla_v]
    for ci in range(tm // L):
        rows = slice(ci * L, (ci + 1) * L)
        _mix_chunk(rest_p[rows, 0:POOL_W], rest_p[rows, POOL_W:POOL_W + M_W], rest_p[rows, POOL_W + M_W:REST_W],
                   vt_p[:, rows], gt_p[:, rows], gc_p[rows, :], (jm % nsb) * tm + ci * L, rows, pm_ref,
                   wpool_ref, spool_ref, wconv_ref, bconv_ref, wmq_ref, wmk_ref, gbias_ref, gbiasr_ref, gmh_ref,
                   xp_buf, xc_buf, c_st, n_st, m_st, fillers)
    while fillers:
        fillers.pop(0)()
    rest_c[...] = vals["rest"]
    vt_c[...] = vals["vt"]
    gt_c[...] = vals["gt"]
    gc_c[...] = vals["gc"]


def _proj(h, ctab, stab, ln0, ln_in, lw, S, tm, L):
    T = h.shape[0]
    nsb = S // tm
    full = lambda a: pl.BlockSpec(a.shape, lambda i: (0,) * a.ndim)
    weights = [ln0[0], ln0[1], lw["wa"], lw["wr"], lw["wvt"], lw["wgt"], lw["gq"], lw["gkv"], lw["wq"], lw["wkv"],
               lw["wpool"], lw["spool"], lw["wconv"], lw["bconv"], lw["wmq"], lw["wmk"], lw["gbias"],
               lw["gbias_row"], lw["gmh"]]
    HW = MLA_HEADS * QK_W
    nt_ = T // tm
    cur = lambda i: jnp.minimum(i, nt_ - 1)
    prev = lambda i: jnp.maximum(i - 1, 0)
    return pl.pallas_call(
        functools.partial(_proj_kernel, nsb=nsb, L=L, ln_in=ln_in),
        grid=(nt_ + 1,),
        in_specs=[pl.BlockSpec((tm, D_MODEL), lambda i: (cur(i), 0)),
                  pl.BlockSpec((tm, LANES), lambda i: (cur(i) % nsb, 0)),
                  pl.BlockSpec((tm, LANES), lambda i: (cur(i) % nsb, 0))] + [full(w) for w in weights],
        out_specs=[pl.BlockSpec((tm, HW), lambda i: (cur(i), 0)),
                   pl.BlockSpec((tm, HW), lambda i: (cur(i), 0)),
                   pl.BlockSpec((tm, HW), lambda i: (cur(i), 0)),
                   pl.BlockSpec((tm, POOL_W + M_W), lambda i: (prev(i), 0))],
        out_shape=[jax.ShapeDtypeStruct((T, HW), BF16),
                   jax.ShapeDtypeStruct((T, HW), BF16),
                   jax.ShapeDtypeStruct((T, HW), BF16),
                   jax.ShapeDtypeStruct((T, POOL_W + M_W), BF16)],
        scratch_shapes=[pltpu.VMEM((POOL_HALO + L, POOL_W), F32),
                        pltpu.VMEM((CONV_HALO + L, M_W), F32),
                        pltpu.VMEM((M_W, M_W), F32),
                        pltpu.VMEM((1, M_W), F32),
                        pltpu.VMEM((2 * M_HEADS, LANES), F32),
                        pltpu.VMEM((tm, REST_W), F32),
                        pltpu.VMEM((M_W, tm), BF16),
                        pltpu.VMEM((GATE_ROWS, tm), F32),
                        pltpu.VMEM((tm, LANES), F32)],
        compiler_params=_cparams(("arbitrary",)),
        name="proj_mix",
    )(h, ctab, stab, *weights)


def _split3(x):
    hi = x.astype(BF16)
    r = x - hi.astype(F32)
    mid = r.astype(BF16)
    lo = (r - mid.astype(F32)).astype(BF16)
    return hi, mid, lo


def _mix_chunk(xp, xqk, xo, vt, gt_raw, gc_raw, tpos0, rows, o_ref,
               wpool_ref, spool_ref, wconv_ref, bconv_ref, wmq_ref, wmk_ref, gbias_ref, gbiasr_ref, gmh_ref,
               xp_buf, xc_buf, c_st, n_st, m_st, fillers):
    L = xp.shape[0]
    H = M_HEADS

    def next_filler():
        if fillers:
            fillers.pop(0)()

    xp_buf[POOL_HALO:POOL_HALO + L, :] = xp
    lane = lax.broadcasted_iota(jnp.int32, (L, POOL_W), 1)
    grp = lane // POOL_CH
    acc = xp_buf[...]
    wsum = jnp.zeros((L, POOL_W), F32)
    for gi, w in enumerate(POOL_WINDOWS):
        assert w == 2 ** (gi + 1)
        acc = acc + pltpu.roll(acc, w // 2, axis=0)
        wsum = jnp.where(grp == gi, acc[POOL_HALO:, :], wsum)
    win = jnp.where(grp == 0, POOL_WINDOWS[0],
                    jnp.where(grp == 1, POOL_WINDOWS[1],
                              jnp.where(grp == 2, POOL_WINDOWS[2], POOL_WINDOWS[3])))
    tpos = tpos0 + lax.broadcasted_iota(jnp.int32, (L, POOL_W), 0)
    cnt = jnp.minimum(tpos + 1, win).astype(F32)
    y = wsum / cnt - xp
    pooled = jnp.dot(y.astype(BF16), wpool_ref[...], preferred_element_type=F32) * spool_ref[...]
    o_ref[rows, 0:POOL_W] = pooled.astype(BF16)
    xp_buf[0:POOL_HALO, :] = xp[L - POOL_HALO:, :]
    next_filler()
    xc_buf[CONV_HALO:CONV_HALO + L, :] = xqk
    conv = xqk * wconv_ref[M_CONV - 1:M_CONV, :] + bconv_ref[...]
    for k in range(1, M_CONV):
        conv = conv + xc_buf[CONV_HALO - k:CONV_HALO - k + L, :] * wconv_ref[M_CONV - 1 - k:M_CONV - k, :]
    xc_buf[0:CONV_HALO, :] = xqk[L - CONV_HALO:, :]
    cs = (conv * jax.nn.sigmoid(conv)).astype(BF16)
    q = jnp.dot(cs, wmq_ref[...], preferred_element_type=F32)
    kk = jnp.dot(cs, wmk_ref[...], preferred_element_type=F32) * (M_HEAD_DIM ** -0.5)
    next_filler()

    tri_r = lax.broadcasted_iota(jnp.int32, (L, L), 0)
    tri_c = lax.broadcasted_iota(jnp.int32, (L, L), 1)
    causal = tri_r <= tri_c

    def cumsum3(lhs_parts, rhs_parts):
        out = None
        for lp, rp in zip(lhs_parts, rhs_parts):
            d = jnp.dot(lp, rp, preferred_element_type=F32)
            out = d if out is None else out + d
        return out

    upper = causal.astype(BF16)
    lower = (tri_c <= tri_r).astype(BF16)
    gt = gt_raw + gbias_ref[...]
    li_r = gt[0:H, :]
    lf_r = jax.nn.log_sigmoid(gt)
    b_r = cumsum3(_split3(lf_r), [upper] * 3)[H:2 * H, :]
    gc = gc_raw + gbiasr_ref[...]
    li_c = gc
    lf_c = jax.nn.log_sigmoid(pltpu.roll(gc, LANES // 2, axis=1))
    b_c = cumsum3([lower] * 3, _split3(lf_c))
    a_c = li_c - b_c

    lane_r = lax.broadcasted_iota(jnp.int32, (1, LANES), 1)
    m_prev = m_st[...]
    e_blocks, g_rows, eneg_rows, mt_last, b_last = [], [], [], [], []
    for hh in range(H):
        a_col = jnp.broadcast_to(a_c[:, hh:hh + 1], (L, L))
        cm = jnp.max(jnp.where(causal, a_col, -jnp.inf), axis=0, keepdims=True)
        mp = jnp.concatenate([m_prev[hh:hh + 1, :]] * (L // LANES), axis=1)
        m_row = jnp.maximum(mp, cm)
        e_blocks.append(jnp.exp(jnp.where(causal, a_col - m_row, -jnp.inf)))
        g_rows.append(jnp.exp(mp - m_row))
        mt = b_r[hh:hh + 1, :] + m_row
        eneg_rows.append(jnp.exp(-mt))
        mt_last.append(mt[:, L - 1:L])
        b_last.append(b_r[hh:hh + 1, L - 1:L])
    ET = jnp.concatenate(e_blocks, axis=0)
    next_filler()

    hlane = lax.broadcasted_iota(jnp.int32, (L, M_W), 1) // M_HEAD_DIM
    kb = kk.astype(BF16)
    qb = q.astype(BF16)
    zero_b = jnp.zeros((L, M_W), BF16)
    k4 = jnp.concatenate([jnp.where(hlane == hh, kb, zero_b) for hh in range(H)], axis=0)
    nt = (((1,), (1,)), ((), ()))
    scT = lax.dot_general(k4, qb, nt, preferred_element_type=F32) * ET
    den_rows = [jnp.sum(scT[hh * L:(hh + 1) * L, :], axis=0, keepdims=True) for hh in range(H)]
    hrow =lax.broadcasted_iota(jnp.int32, (M_W, L), 0) // M_HEAD_DIM
    zero_v = jnp.zeros((M_W, L), BF16)
    vt4 = jnp.concatenate([jnp.where(hrow == hh, vt, zero_v) for hh in range(H)], axis=1)
    numT = jnp.dot(vt4, scT.astype(BF16), preferred_element_type=F32)
    ct_prev = c_st[...]
    interT = lax.dot_general(ct_prev.astype(BF16), qb, nt, preferred_element_type=F32)
    n_prev = n_st[...]
    nrow = lax.broadcasted_iota(jnp.int32, (GATE_ROWS, M_W), 0)
    nlane = lax.broadcasted_iota(jnp.int32, (GATE_ROWS, M_W), 1) // M_HEAD_DIM
    n4 = jnp.where(nrow == nlane, jnp.broadcast_to(n_prev, (GATE_ROWS, M_W)), 0.0).astype(BF16)
    qn = lax.dot_general(n4, qb, nt, preferred_element_type=F32)

    y_blocks = []
    for hh in range(H):
        erows = slice(hh * M_HEAD_DIM, (hh + 1) * M_HEAD_DIM)
        den = g_rows[hh] * qn[hh:hh + 1, :] + den_rows[hh]
        rinv = 1.0 / jnp.maximum(jnp.abs(den), eneg_rows[hh])
        hblk = (g_rows[hh] * interT[erows, :] + numT[erows, :]) * rinv
        mu = jnp.mean(hblk, axis=0, keepdims=True)
        dd = hblk - mu
        var = jnp.mean(dd * dd, axis=0, keepdims=True)
        y_blocks.append(dd * lax.rsqrt(var + LN_EPS))
    yn = jnp.concatenate(y_blocks, axis=0).T
    o_ref[rows, POOL_W:POOL_W + M_W] = (jax.nn.sigmoid(xo) * (yn * gmh_ref[...])).astype(BF16)
    next_filler()

    lane_h = lax.broadcasted_iota(jnp.int32, (1, M_W), 1) // M_HEAD_DIM
    hv = jnp.zeros((1, LANES), F32)
    dec_row = jnp.zeros((1, M_W), F32)
    for hh in range(H):
        m_new = mt_last[hh]
        hv = jnp.where(lane_r == hh, b_last[hh] - m_new, hv)
        decay = jnp.exp(b_last[hh] + m_prev[hh:hh + 1, 0:1] - m_new)
        dec_row = jnp.where(lane_h == hh, decay, dec_row)
        m_st[hh:hh + 1, :] = jnp.broadcast_to(m_new, (1, LANES))
    w_c = jnp.exp(a_c + hv)
    wexp = jnp.zeros((L, M_W), F32)
    for hh in range(H):
        wexp = jnp.where(hlane == hh, jnp.broadcast_to(w_c[:, hh:hh + 1], (L, M_W)), wexp)
    kw = kk * wexp
    upd = jnp.dot(vt, kw.astype(BF16), preferred_element_type=F32)
    bd = (lax.broadcasted_iota(jnp.int32, (M_W, M_W), 0) // M_HEAD_DIM
          == lax.broadcasted_iota(jnp.int32, (M_W, M_W), 1) // M_HEAD_DIM)
    c_st[...] = jnp.where(bd, dec_row * ct_prev + upd, 0.0)
    n_st[...] = dec_row * n_prev + jnp.sum(kw, axis=0, keepdims=True)


def _attn_kernel(q_ref, k_ref, v_ref, o_ref, m_sc, acc_sc, *, t, nsub):
    qi = pl.program_id(2)
    for r in range(nsub):
        m_sc[r] = jnp.full((t, LANES), -jnp.inf, F32)
        acc_sc[r] = jnp.zeros((t, 2 * MLA_V), F32)

    def chain_step(r, start, tk, masked):
        q = q_ref[r * t:(r + 1) * t, :]
        k = k_ref[pl.ds(start, tk), :]
        v = v_ref[pl.ds(start, tk), :]
        s = lax.dot_general(q, k, (((1,), (1,)), ((), ())), preferred_element_type=F32)
        if masked:
            ri = lax.broadcasted_iota(jnp.int32, (t, tk), 0)
            ci = lax.broadcasted_iota(jnp.int32, (t, tk), 1)
            s = jnp.where(ci <= ri, s, -jnp.inf)
        m_old = m_sc[r]
        m_new = jnp.maximum(m_old, jnp.max(s, axis=-1, keepdims=True))
        p = jnp.exp2(s - jnp.concatenate([m_new] * (tk // LANES), axis=1))
        alpha = jnp.exp2(m_old - m_new)
        pv = jnp.dot(p.astype(BF16), v, preferred_element_type=F32)
        acc_sc[r] = jnp.concatenate([alpha, alpha], axis=1) * acc_sc[r] + pv
        m_sc[r] = m_new

    def body(i, carry):
        for jj in range(nsub):
            for r in range(nsub):
                chain_step(r, pl.multiple_of((nsub * i + jj) * t, t), t, False)
        return carry

    lax.fori_loop(0, qi, body, 0)
    for kt in range(nsub):
        for r in range(kt, nsub):
            chain_step(r, pl.multiple_of((nsub * qi + kt) * t, t), t, r == kt)
    for r in range(nsub):
        acc = acc_sc[r]
        o_ref[r * t:(r + 1) * t, :] = (acc[:, :MLA_V] / acc[:, MLA_V:]).astype(o_ref.dtype)


def _attention(qc, kc, vv, B, S, t, nsub):
    T = qc.shape[0]
    tq = t * nsub
    nq = S // tq
    return pl.pallas_call(
        functools.partial(_attn_kernel, t=t, nsub=nsub),
        grid=(B, MLA_HEADS, nq),
        in_specs=[pl.BlockSpec((tq, QK_W), lambda b, h, i: (b * nq + i, h)),
                  pl.BlockSpec((S, QK_W), lambda b, h, i: (b, h)),
                  pl.BlockSpec((S, 2 * MLA_V), lambda b, h, i: (b, h))],
        out_specs=pl.BlockSpec((tq, MLA_V), lambda b, h, i: (b * nq + i, h)),
        out_shape=jax.ShapeDtypeStruct((T, MLA_HEADS * MLA_V), BF16),
        scratch_shapes=[pltpu.VMEM((nsub, t, LANES), F32), pltpu.VMEM((nsub, t, 2 * MLA_V), F32)],
        compiler_params=_cparams(("arbitrary", "arbitrary", "arbitrary")),
        name="mla_attention",
    )(qc, kc, vv)


def _block_kernel(h_ref, a_ref, pm_ref, g0_ref, b0_ref, woa_ref, wopm_ref, g1_ref, b1_ref,
                  wa_ref, wg_ref, wfc_ref, bfc_ref, wd_ref, g2_ref, b2_ref, o_ref,
                  tail_sc, u_buf, *, ts, nsub, tf, nsb, ln_in):
    i = pl.program_id(0)
    nchunk = D_FF // tf

    @pl.when((i % nsb) == 0)
    def _():
        tail_sc[...] = jnp.zeros((CONV_HALO, D_FF), F32)

    row8 = lax.broadcasted_iota(jnp.int32, (CONV_HALO, tf), 0)
    h1s, hbs = [], []
    for r in range(nsub):
        rows = slice(r * ts, (r + 1) * ts)
        mix = (jnp.dot(a_ref[rows, :], woa_ref[...], preferred_element_type=F32)
               + jnp.dot(pm_ref[rows, :], wopm_ref[...], preferred_element_type=F32))
        h = h_ref[rows, :]
        if ln_in:
            h = _layer_norm_rows(h, g0_ref[...], b0_ref[...])
        h1 = _layer_norm_rows(ALPHA * h + mix, g1_ref[...], b1_ref[...])
        h1s.append(h1)
        hbs.append(h1.astype(BF16))

    def up(r, c):
        cols = slice(c * tf, (c + 1) * tf)
        g = jnp.dot(hbs[r], wg_ref[:, cols], preferred_element_type=F32)
        a = jnp.dot(hbs[r], wa_ref[:, cols], preferred_element_type=F32)
        return g, a

    def shifted(g, prev8, k):
        body = pltpu.roll(g, k, axis=0)
        top = jnp.where(row8 < k, pltpu.roll(prev8, k, axis=0), body[0:CONV_HALO, :])
        return jnp.concatenate([top, body[CONV_HALO:, :]], axis=0)

    order = [(r, c) for c in range(nchunk) for r in range(nsub)]
    nxt = up(*order[0])
    for n, (r, c) in enumerate(order):
        cols = slice(c * tf, (c + 1) * tf)
        g, a = nxt
        if n + 1 < len(order):
            nxt = up(*order[n + 1])
        prev8 = tail_sc[:, cols]
        tail_sc[:, cols] = g[ts - CONV_HALO:, :]
        conv = g * wfc_ref[FFN_CONV - 1:FFN_CONV, cols] + bfc_ref[:, cols]
        for k in range(1, FFN_CONV):
            conv = conv + shifted(g, prev8, k) * wfc_ref[FFN_CONV - 1 - k:FFN_CONV - k, cols]
        u_buf[r, :, cols] = (conv * jax.nn.sigmoid(conv) * a).astype(BF16)
    for r in range(nsub):
        f = jnp.dot(u_buf[r], wd_ref[...], preferred_element_type=F32)
        o_ref[r * ts:(r + 1) * ts, :] = _layer_norm_rows(ALPHA * h1s[r] + f, g2_ref[...], b2_ref[...])


def _block(h, a, pm, ln0, ln_in, lw, S, ts, nsub, tf):
    T = h.shape[0]
    tm = ts * nsub
    nsb = S // tm
    const = lambda x: pl.BlockSpec(x.shape, lambda i: (0,) * x.ndim, pipeline_mode=pl.Buffered(1))
    weights = [ln0[0], ln0[1], lw["woa"], lw["wopm"], lw["ln1g"], lw["ln1b"], lw["wup_a"], lw["wup_g"], lw["wfc"],
               lw["bfc"], lw["wd"], lw["ln2g"], lw["ln2b"]]
    return pl.pallas_call(
        functools.partial(_block_kernel, ts=ts, nsub=nsub, tf=tf, nsb=nsb, ln_in=ln_in),
        grid=(T // tm,),
        in_specs=[pl.BlockSpec((tm, D_MODEL), lambda i: (i, 0)),
                  pl.BlockSpec((tm, a.shape[1]), lambda i: (i, 0)),
                  pl.BlockSpec((tm, pm.shape[1]), lambda i: (i, 0))] + [const(w) for w in weights],
        out_specs=pl.BlockSpec((tm, D_MODEL), lambda i: (i, 0)),
        out_shape=jax.ShapeDtypeStruct((T, D_MODEL), F32),
        scratch_shapes=[pltpu.VMEM((CONV_HALO, D_FF), F32),
                        pltpu.VMEM((nsub, ts, D_FF), BF16)],
        compiler_params=_cparams(("arbitrary",)),
        name="outproj_ffn",
    )(h, a, pm, *weights)


def _block_diag(w):
    G, C, _ = w.shape
    out = jnp.zeros((G * C, G * C), w.dtype)
    for g in range(G):
        out = out.at[g * C:(g + 1) * C, g * C:(g + 1) * C].set(w[g])
    return out


def _prep_layer(l, w_in, g_qn, w_uq, g_kvn, w_ukv, w_pool, s_pool, w_mconv, b_mconv, w_mq, w_mk,
                b_i, b_f, g_mh, w_out, ln1_g, ln1_b, w_up, w_fconv, b_fconv, w_down, ln2_g, ln2_b):
    half = MLA_ROPE // 2
    wi = w_in[l]
    o_kr = Q_LORA + KV_LORA
    o_rest = o_kr + MLA_ROPE
    o_gate = o_rest + 4 * 256
    kr = wi[:, o_kr:o_kr + MLA_ROPE]
    zpad = jnp.zeros((D_MODEL, LANES - MLA_ROPE), F32)
    kr_sw = jnp.concatenate([-kr[:, half:], kr[:, :half]], axis=1)
    wgate = wi[:, o_gate:o_gate + 2 * M_HEADS]
    zg = jnp.zeros((D_MODEL, LANES // 2 - M_HEADS), F32)
    wgc = jnp.concatenate([wgate[:, :M_HEADS], zg, wgate[:, M_HEADS:], zg], axis=1)
    wa = jnp.concatenate([wi[:, :Q_LORA], wgc, wi[:, Q_LORA:o_kr], kr, zpad, kr_sw, zpad], axis=1)
    o_xv = o_rest + POOL_W + M_W
    wr = jnp.concatenate([wi[:, o_rest:o_xv], wi[:, o_xv + M_W:o_gate]], axis=1)
    wvt = wi[:, o_xv:o_xv + M_W].T
    wgt = jnp.concatenate([wgate.T, jnp.zeros((GATE_ROWS - 2 * M_HEADS, D_MODEL), F32)], axis=0)
    zb = jnp.zeros((LANES // 2 - M_HEADS,), F32)
    gbias_row = jnp.concatenate([b_i[l], zb, b_f[l], zb]).reshape(1, LANES)

    wuq = w_uq[l].reshape(Q_LORA, MLA_HEADS, MLA_NOPE + MLA_ROPE)
    zq = jnp.zeros((Q_LORA, MLA_HEADS, LANES - MLA_ROPE), F32)
    rope = wuq[:, :, MLA_NOPE:]
    wq = jnp.concatenate([wuq, zq, -rope[:, :, half:], rope[:, :, :half], zq],
                         axis=2).reshape(Q_LORA, MLA_HEADS * Q_HEAD_COLS)

    wukv = w_ukv[l].reshape(KV_LORA, MLA_HEADS, MLA_NOPE + MLA_V)
    wkv = jnp.concatenate([wukv[:, :, :MLA_NOPE].reshape(KV_LORA, -1),
                           wukv[:, :, MLA_NOPE:].reshape(KV_LORA, -1)], axis=1)

    gbias = jnp.concatenate([b_i[l], b_f[l], jnp.zeros((GATE_ROWS - 2 * M_HEADS,), F32)]).reshape(GATE_ROWS, 1)
    wo = w_out[l]
    return dict(
        wa=wa.astype(BF16), wr=wr.astype(BF16), wvt=wvt.astype(BF16), wgt=wgt.astype(BF16),
        gbias_row=gbias_row,
        gq=g_qn[l].reshape(1, Q_LORA), gkv=g_kvn[l].reshape(1, KV_LORA),
        wq=wq.astype(BF16), wkv=wkv.astype(BF16),
        wpool=_block_diag(w_pool[l]).astype(BF16), spool=s_pool[l].reshape(1, POOL_W),
        wconv=w_mconv[l], bconv=b_mconv[l].reshape(1, M_W),
        wmq=_block_diag(w_mq[l]).astype(BF16), wmk=_block_diag(w_mk[l]).astype(BF16),
        gbias=gbias, gmh=g_mh[l].reshape(1, M_W),
        woa=wo[:MLA_HEADS * MLA_V].astype(BF16), wopm=wo[MLA_HEADS * MLA_V:].astype(BF16),
        ln1g=ln1_g[l].reshape(1, D_MODEL), ln1b=ln1_b[l].reshape(1, D_MODEL),
        wup_a=w_up[l][:, :D_FF].astype(BF16), wup_g=w_up[l][:, D_FF:].astype(BF16),
        wfc=w_fconv[l], bfc=b_fconv[l].reshape(1, D_FF), wd=w_down[l].astype(BF16),
        ln2g=ln2_g[l].reshape(1, D_MODEL), ln2b=ln2_b[l].reshape(1, D_MODEL),
    )


def _rope_tables(S):
    inv = 1.0 / (ROPE_BASE ** (jnp.arange(0, MLA_ROPE, 2, dtype=F32) / MLA_ROPE))
    ang = jnp.arange(S, dtype=F32)[:, None] * inv[None, :]
    z = jnp.zeros((S, LANES - MLA_ROPE), F32)
    ctab = jnp.concatenate([jnp.cos(ang), jnp.cos(ang), z], axis=1)
    stab = jnp.concatenate([jnp.sin(ang), jnp.sin(ang), z], axis=1)
    return ctab, stab


def _tile(pref, n):
    t = min(pref, n)
    assert n % t == 0, (pref, n)
    return t


def kernel(x, ln0_g, ln0_b, w_in, g_qn, w_uq, g_kvn, w_ukv, w_pool, s_pool, w_mconv, b_mconv, w_mq, w_mk, b_i, b_f, g_mh, w_out, ln1_g, ln1_b, w_up, w_fconv, b_fconv, w_down, ln2_g, ln2_b):
    B, S, D = x.shape
    assert D == D_MODEL
    T = B * S
    params = (w_in, g_qn, w_uq, g_kvn, w_ukv, w_pool, s_pool, w_mconv, b_mconv, w_mq, w_mk, b_i, b_f,
              g_mh, w_out, ln1_g, ln1_b, w_up, w_fconv, b_fconv, w_down, ln2_g, ln2_b)
    ctab, stab = _rope_tables(S)
    ln0 = (ln0_g.reshape(1, D), ln0_b.reshape(1, D))
    h = x.reshape(T, D)
    for l in range(DEPTH):
        lw = _prep_layer(l, *params)
        tm = _tile(PROJ_TM, S)
        first = l == 0
        qc, kc, vv, pm = _proj(h, ctab, stab, ln0, first, lw, S, tm, _tile(MIX_L, tm))
        a = _attention(qc, kc, vv, B, S, _tile(ATT_T, S // ATT_NSUB), ATT_NSUB)
        h = _block(h, a, pm, ln0, first, lw, S, _tile(FFN_TM, S // FFN_NSUB), FFN_NSUB, FFN_TF)
    return h.reshape(B, S, D)
```
